```python
import jax, jax.numpy as jnp
from jax import lax
import numpy as np

D_MODEL = 2048
BATCH = 4
SEQ = 2048
DEPTH = 2
DEC_BATCH = 128
DEC_SEQ = 1
PAST_LEN = 2048
PAGE_SIZE = 128

N_META = 16
MIX_WIDTH = D_MODEL
ATTN_WIDTH = MIX_WIDTH // 2
POOL_WIDTH = MIX_WIDTH - ATTN_WIDTH
HEAD_DIM = 128
N_HEADS = ATTN_WIDTH // HEAD_DIM
POOL_WINDOWS = (2, 4, 8, 16)
N_POOL_GROUPS = len(POOL_WINDOWS)
POOL_GROUP_WIDTH = POOL_WIDTH // N_POOL_GROUPS
POOL_CTX = max(POOL_WINDOWS) - 1
D_FF = ((8 * D_MODEL // 3 + 127) // 128) * 128
PROJ_WIDTH = 3 * ATTN_WIDTH + POOL_WIDTH + N_HEADS
Q_BLOCK = 128
RMS_EPS = 1e-6
NEG_INF = -1e30
ATTN_SCALE = HEAD_DIM ** -0.5

kernel_name = 'hymba_fox_pool_macaron_step'


def rms_norm(x, g):
    xf = x.astype(jnp.float32)
    y = xf * lax.rsqrt(jnp.mean(xf * xf, axis=-1, keepdims=True) + RMS_EPS)
    return (y * g.astype(jnp.float32)).astype(x.dtype)


def swiglu(u, wg, wu, wd):
    a = jnp.einsum('btd,df->btf', u, wg)
    b = jnp.einsum('btd,df->btf', u, wu)
    return jnp.einsum('btf,fd->btd', jax.nn.silu(a) * b, wd)


def ffn_half(h, g_pre, g_post, wg, wu, wd):
    return h + 0.5 * rms_norm(swiglu(rms_norm(h, g_pre), wg, wu, wd), g_post)


def in_proj(u, w_in, b_f):
    z = jnp.einsum('btd,de->bte', u, w_in)
    B, T, _ = z.shape
    A = ATTN_WIDTH
    q = z[..., :A].reshape(B, T, N_HEADS, HEAD_DIM)
    k = z[..., A:2 * A].reshape(B, T, N_HEADS, HEAD_DIM)
    v = z[..., 2 * A:3 * A].reshape(B, T, N_HEADS, HEAD_DIM)
    p = z[..., 3 * A:3 * A + POOL_WIDTH]
    fl = z[..., 3 * A + POOL_WIDTH:]
    logf = jax.nn.log_sigmoid(fl.astype(jnp.float32) + b_f.astype(jnp.float32))
    return q, k, v, p, logf


def fox_attend(q, k, v, Fq, Fk, qpos, kpos):
    s = jnp.einsum('bqhd,bkhd->bhqk', q, k).astype(jnp.float32) * ATTN_SCALE
    bias = jnp.transpose(Fq, (0, 2, 1))[..., :, None] - jnp.transpose(Fk, (0, 2, 1))[..., None, :]
    s = s + bias
    mask = kpos[None, :] <= qpos[:, None]
    s = jnp.where(mask[None, None], s, NEG_INF)
    pr = jax.nn.softmax(s, axis=-1)
    return jnp.einsum('bhqk,bkhd->bqhd', pr.astype(v.dtype), v)


def prompt_attention(q, k, v, F):
    B, L, H, D = q.shape
    meta_pos = jnp.arange(N_META)
    o_meta = fox_attend(q[:, :N_META], k[:, :N_META], v[:, :N_META],
                        F[:, :N_META], F[:, :N_META], meta_pos, meta_pos)
    n_real = L - N_META
    nblk = n_real // Q_BLOCK
    qr = q[:, N_META:].reshape(B, nblk, Q_BLOCK, H, D).transpose(1, 0, 2, 3, 4)
    Fr = F[:, N_META:].reshape(B, nblk, Q_BLOCK, H).transpose(1, 0, 2, 3)
    kpos = jnp.arange(L)

    def one_block(args):
        qb, Fb, i = args
        qpos = N_META + i * Q_BLOCK + jnp.arange(Q_BLOCK)
        return fox_attend(qb, k, v, Fb, F, qpos, kpos)

    o = lax.map(one_block, (qr, Fr, jnp.arange(nblk)))
    o = o.transpose(1, 0, 2, 3, 4).reshape(B, n_real, H, D)
    return jnp.concatenate([o_meta, o], axis=1).reshape(B, L, H * D)


def pool_mix(ext, n_out, w_pool, scale):
    B, Lx, C = ext.shape
    c = jnp.cumsum(ext.astype(jnp.float32), axis=1)
    c = jnp.concatenate([jnp.zeros((B, 1, C), jnp.float32), c], axis=1)
    i = jnp.arange(Lx - n_out, Lx)
    hi = c[:, Lx - n_out + 1:Lx + 1]
    cur = ext[:, Lx - n_out:].astype(jnp.float32)
    outs = []
    for g, w in enumerate(POOL_WINDOWS):
        sl = slice(g * POOL_GROUP_WIDTH, (g + 1) * POOL_GROUP_WIDTH)
        lo = jnp.maximum(i + 1 - w, 0)
        cnt = (i + 1 - lo).astype(jnp.float32)
        mean = (hi[..., sl] - c[:, lo, sl]) / cnt[None, :, None]
        outs.append(mean - cur[..., sl])
    d = jnp.stack(outs, axis=2).astype(ext.dtype)
    y = jnp.einsum('btgc,gcd->btgd', d, w_pool).reshape(B, n_out, C)
    return y * scale


def setup_inputs(seed: int = 0) -> dict:
    key = jax.random.key(seed)
    ks = jax.random.split(key, 32)
    n_pages = PAST_LEN // PAGE_SIZE
    n_phys = (DEC_BATCH * n_pages * 5) // 4
    f32 = jnp.float32

    def nrm(k, shape, s):
        return jax.random.normal(k, shape, f32) * s

    def gain(k):
        return 1.0 + 0.05 * jax.random.normal(k, (DEPTH, D_MODEL), f32)

    perm = jax.random.permutation(ks[6], n_phys)
    page_table = perm[:DEC_BATCH * n_pages].reshape(DEC_BATCH, n_pages).astype(jnp.int32)
    return {
        'x_prompt': nrm(ks[0], (BATCH, SEQ, D_MODEL), 1.0),
        'x_sample': nrm(ks[1], (DEC_BATCH, DEC_SEQ, D_MODEL), 1.0),
        'cache_k': nrm(ks[2], (DEPTH, n_phys, PAGE_SIZE, N_HEADS, HEAD_DIM), 1.0),
        'cache_v': nrm(ks[3], (DEPTH, n_phys, PAGE_SIZE, N_HEADS, HEAD_DIM), 1.0),
        'cache_logf': jax.nn.log_sigmoid(2.0 + jax.random.normal(ks[4], (DEPTH, n_phys, PAGE_SIZE, N_HEADS), f32)),
        'state_pool': nrm(ks[5], (DEPTH, DEC_BATCH, POOL_CTX, POOL_WIDTH), 1.0),
        'page_table': page_table,
        'meta_tokens': nrm(ks[7], (N_META, D_MODEL), 1.0),
        'w_in': nrm(ks[8], (DEPTH, D_MODEL, PROJ_WIDTH), D_MODEL ** -0.5),
        'b_forget': 2.0 + 0.1 * jax.random.normal(ks[9], (DEPTH, N_HEADS), f32),
        'w_pool': nrm(ks[10], (DEPTH, N_POOL_GROUPS, POOL_GROUP_WIDTH, POOL_GROUP_WIDTH), POOL_GROUP_WIDTH ** -0.5),
        'pool_scale': 1.0 + 0.1 * jax.random.normal(ks[11], (DEPTH, POOL_WIDTH), f32),
        'w_out': nrm(ks[12], (DEPTH, MIX_WIDTH, D_MODEL), MIX_WIDTH ** -0.5),
        'ffn1_gate': nrm(ks[13], (DEPTH, D_MODEL, D_FF), D_MODEL ** -0.5),
        'ffn1_up': nrm(ks[14], (DEPTH, D_MODEL, D_FF), D_MODEL ** -0.5),
        'ffn1_down': nrm(ks[15], (DEPTH, D_FF, D_MODEL), D_FF ** -0.5),
        'ffn2_gate': nrm(ks[16], (DEPTH, D_MODEL, D_FF), D_MODEL ** -0.5),
        'ffn2_up': nrm(ks[17], (DEPTH, D_MODEL, D_FF), D_MODEL ** -0.5),
        'ffn2_down': nrm(ks[18], (DEPTH, D_FF, D_MODEL), D_FF ** -0.5),
        'g_pre_ffn1': gain(ks[19]),
        'g_post_ffn1': gain(ks[20]),
        'g_pre_mix': gain(ks[21]),
        'g_post_mix': gain(ks[22]),
        'g_pre_ffn2': gain(ks[23]),
        'g_post_ffn2': gain(ks[24]),
    }


def reference(x_prompt, x_sample, cache_k, cache_v, cache_logf, state_pool, page_table, meta_tokens,
              w_in, b_forget, w_pool, pool_scale, w_out,
              ffn1_gate, ffn1_up, ffn1_down, ffn2_gate, ffn2_up, ffn2_down,
              g_pre_ffn1, g_post_ffn1, g_pre_mix, g_post_mix, g_pre_ffn2, g_post_ffn2):
    B = x_prompt.shape[0]
    DB = x_sample.shape[0]
    T = x_sample.shape[1]
    past = page_table.shape[1] * PAGE_SIZE
    meta = jnp.broadcast_to(meta_tokens[None].astype(x_prompt.dtype), (B, N_META, D_MODEL))
    hp = jnp.concatenate([meta, x_prompt], axis=1)
    L = hp.shape[1]
    hs = x_sample
    kp_l, vp_l, lfp_l, pp_l = [], [], [], []
    ks_l, vs_l, lfs_l, ps_l = [], [], [], []
    for l in range(DEPTH):
        hp = ffn_half(hp, g_pre_ffn1[l], g_post_ffn1[l], ffn1_gate[l], ffn1_up[l], ffn1_down[l])
        hs = ffn_half(hs, g_pre_ffn1[l], g_post_ffn1[l], ffn1_gate[l], ffn1_up[l], ffn1_down[l])

        up = rms_norm(hp, g_pre_mix[l])
        q, k, v, p, lf = in_proj(up, w_in[l], b_forget[l])
        F = jnp.cumsum(lf, axis=1)
        a = prompt_attention(q, k, v, F)
        pm = pool_mix(p, L, w_pool[l], pool_scale[l])
        mix = jnp.einsum('bte,ed->btd', jnp.concatenate([a, pm.astype(a.dtype)], axis=-1), w_out[l])
        hp = hp + rms_norm(mix, g_post_mix[l])
        kp_l.append(k); vp_l.append(v); lfp_l.append(lf); pp_l.append(p[:, L - POOL_CTX:])

        us = rms_norm(hs, g_pre_mix[l])
        q, k, v, p, lf = in_proj(us, w_in[l], b_forget[l])
        k_past = cache_k[l][page_table].reshape(DB, past, N_HEADS, HEAD_DIM).astype(k.dtype)
        v_past = cache_v[l][page_table].reshape(DB, past, N_HEADS, HEAD_DIM).astype(v.dtype)
        lf_past = cache_logf[l][page_table].reshape(DB, past, N_HEADS).astype(jnp.float32)
        k_all = jnp.concatenate([k_past, k], axis=1)
        v_all = jnp.concatenate([v_past, v], axis=1)
        F = jnp.cumsum(jnp.concatenate([lf_past, lf], axis=1), axis=1)
        a = fox_attend(q, k_all, v_all, F[:, past:], F, past + jnp.arange(T), jnp.arange(past + T))
        a = a.reshape(DB, T, ATTN_WIDTH)
        ext = jnp.concatenate([state_pool[l].astype(p.dtype), p], axis=1)
        pm = pool_mix(ext, T, w_pool[l], pool_scale[l])
        mix = jnp.einsum('bte,ed->btd', jnp.concatenate([a, pm.astype(a.dtype)], axis=-1), w_out[l])
        hs = hs + rms_norm(mix, g_post_mix[l])
        ks_l.append(k); vs_l.append(v); lfs_l.append(lf); ps_l.append(ext[:, ext.shape[1] - POOL_CTX:])

        hp = ffn_half(hp, g_pre_ffn2[l], g_post_ffn2[l], ffn2_gate[l], ffn2_up[l], ffn2_down[l])
        hs = ffn_half(hs, g_pre_ffn2[l], g_post_ffn2[l], ffn2_gate[l], ffn2_up[l], ffn2_down[l])

    y_prompt = hp[:, N_META:]
    y_sample = hs
    k_prompt = jnp.stack(kp_l, 0)
    v_prompt = jnp.stack(vp_l, 0)
    logf_prompt = jnp.stack(lfp_l, 0)
    pool_prompt = jnp.stack(pp_l, 0)
    k_sample = jnp.stack(ks_l, 0)
    v_sample = jnp.stack(vs_l, 0)
    logf_sample = jnp.stack(lfs_l, 0)
    pool_sample = jnp.stack(ps_l, 0)
    return (y_prompt, y_sample, k_prompt, v_prompt, logf_prompt, pool_prompt, k_sample, v_sample, logf_sample, pool_sample)
```

```python
import functools

import jax
import jax.numpy as jnp
from jax import lax
from jax.experimental import pallas as pl
from jax.experimental.pallas import tpu as pltpu

F32 = jnp.float32
BF16 = jnp.bfloat16

N_META = 16
HEAD_DIM = 128
PAGE_SIZE = 128
POOL_WINDOWS = (2, 4, 8, 16)
POOL_CTX = max(POOL_WINDOWS) - 1
RMS_EPS = 1e-6
NEG_INF = -1e30
ATTN_SCALE = HEAD_DIM ** -0.5

LANE = 128
ROW_TILE = 704
ROW_ALIGN = 8448
FF_TILE = 512
POOL_TILE = 512
Q_TILE = 256
VMEM_LIMIT = 56 * 1024 * 1024


def _rms(x, g):
    return x * lax.rsqrt(jnp.mean(x * x, axis=-1, keepdims=True) + RMS_EPS) * g


def _split3(x):
    hi = x.astype(BF16)
    r1 = x - hi.astype(F32)
    mid = r1.astype(BF16)
    lo = (r1 - mid.astype(F32)).astype(BF16)
    return hi, mid, lo


def _dot3(x, w01):
    hi, mid, lo = _split3(x)
    acc = jnp.dot(hi, w01, preferred_element_type=F32)
    acc = acc + jnp.dot(mid, w01, preferred_element_type=F32)
    return acc + jnp.dot(lo, w01, preferred_element_type=F32)


def _dot_nt(a, b):
    return lax.dot_general(a, b, (((1,), (1,)), ((), ())), preferred_element_type=F32)


def _ffn_kernel(x_ref, gpre_ref, gpost_ref, wg_ref, wu_ref, wd_ref, o_ref, xn_ref):
    j = pl.program_id(1)

    @pl.when(j == 0)
    def _():
        xn_ref[...] = _rms(x_ref[...], gpre_ref[...]).astype(BF16)

    xn = xn_ref[...]
    a = jnp.dot(xn, wg_ref[...], preferred_element_type=F32)
    b = jnp.dot(xn, wu_ref[...], preferred_element_type=F32)
    h = (a * jax.nn.sigmoid(a) * b).astype(BF16)
    part = jnp.dot(h, wd_ref[...], preferred_element_type=F32)

    @pl.when(j == 0)
    def _():
        o_ref[...] = part

    @pl.when(j > 0)
    def _():
        o_ref[...] += part

    @pl.when(j == pl.num_programs(1) - 1)
    def _():
        o_ref[...] = x_ref[...] + 0.5 * _rms(o_ref[...], gpost_ref[...])


def _ffn_half(h, g_pre, g_post, wg, wu, wd):
    rows, d = h.shape
    ff = wg.shape[1]
    return pl.pallas_call(
        _ffn_kernel,
        grid=(rows // ROW_TILE, ff // FF_TILE),
        in_specs=[
            pl.BlockSpec((ROW_TILE, d), lambda i, j: (i, 0)),
            pl.BlockSpec((1, d), lambda i, j: (0, 0)),
            pl.BlockSpec((1, d), lambda i, j: (0, 0)),
            pl.BlockSpec((d, FF_TILE), lambda i, j: (0, j)),
            pl.BlockSpec((d, FF_TILE), lambda i, j: (0, j)),
            pl.BlockSpec((FF_TILE, d), lambda i, j: (j, 0)),
        ],
        out_specs=pl.BlockSpec((ROW_TILE, d), lambda i, j: (i, 0)),
        out_shape=jax.ShapeDtypeStruct((rows, d), F32),
        scratch_shapes=[pltpu.VMEM((ROW_TILE, d), BF16)],
        compiler_params=pltpu.CompilerParams(
            dimension_semantics=("parallel", "arbitrary"), vmem_limit_bytes=VMEM_LIMIT),
        name="ffn_half",
    )(h, g_pre, g_post, wg, wu, wd)


def _inproj_kernel(h_ref, g_ref, w_ref, wf_ref, bf_ref, z_ref, lf_ref, u_ref):
    n = pl.program_id(1)

    @pl.when(n == 0)
    def _():
        u = _rms(h_ref[...], g_ref[...]).astype(BF16)
        u_ref[...] = u
        x = jnp.dot(u, wf_ref[...], preferred_element_type=F32) + bf_ref[...]
        lf = jnp.minimum(x, 0.0) - jnp.log1p(jnp.exp(-jnp.abs(x)))
        lf_ref[...] = lf[:, :lf_ref.shape[1]]

    z_ref[0] = jnp.dot(u_ref[...], w_ref[...], preferred_element_type=F32)


def _in_proj(h, g, w_main, w_f, b_f, n_heads):
    rows, d = h.shape
    width = w_main.shape[1]
    col_tile = width // 4
    return pl.pallas_call(
        _inproj_kernel,
        grid=(rows // ROW_TILE, 4),
        in_specs=[
            pl.BlockSpec((ROW_TILE, d), lambda i, n: (i, 0)),
            pl.BlockSpec((1, d), lambda i, n: (0, 0)),
            pl.BlockSpec((d, col_tile), lambda i, n: (0, n)),
            pl.BlockSpec((d, LANE), lambda i, n: (0, 0)),
            pl.BlockSpec((1, LANE), lambda i, n: (0, 0)),
        ],
        out_specs=[
            pl.BlockSpec((1, ROW_TILE, col_tile), lambda i, n: (n, i, 0)),
            pl.BlockSpec((ROW_TILE, n_heads), lambda i, n: (i, 0)),
        ],
        out_shape=[
            jax.ShapeDtypeStruct((4, rows, col_tile), F32),
            jax.ShapeDtypeStruct((rows, n_heads), F32),
        ],
        scratch_shapes=[pltpu.VMEM((ROW_TILE, d), BF16)],
        compiler_params=pltpu.CompilerParams(
            dimension_semantics=("parallel", "arbitrary"), vmem_limit_bytes=VMEM_LIMIT),
        name="in_proj",
    )(h, g, w_main, w_f, b_f)


def _upper_ones(n):
    r = lax.broadcasted_iota(jnp.int32, (n, n), 0)
    c = lax.broadcasted_iota(jnp.int32, (n, n), 1)
    return jnp.where(r <= c, 1.0, 0.0).astype(BF16)


def _fcum_kernel(lfm_ref, lfr_ref, fm_ref, fr_ref):
    u = _upper_ones(LANE)
    fm = _dot3(lfm_ref[...], u)
    fm_ref[...] = fm
    carry = fm[:, LANE - 1:LANE]
    for blk in range(lfr_ref.shape[1] // LANE):
        sl = slice(blk * LANE, (blk + 1) * LANE)
        f = _dot3(lfr_ref[:, sl], u) + carry
        fr_ref[:, sl] = f
        carry = f[:, LANE - 1:LANE]


def _prompt_fcum(lf_meta_t, lf_real_t):
    return pl.pallas_call(
        _fcum_kernel,
        out_shape=[jax.ShapeDtypeStruct(lf_meta_t.shape, F32), jax.ShapeDtypeStruct(lf_real_t.shape, F32)],
        name="prompt_fcum",
    )(lf_meta_t, lf_real_t)


def _softmax_pv(parts):
    m = None
    for s, _ in parts:
        mi = jnp.max(s, axis=-1, keepdims=True)
        m = mi if m is None else jnp.maximum(m, mi)
    l = None
    o = None
    for s, v in parts:
        p = jnp.exp(s - m)
        li = jnp.sum(p, axis=-1, keepdims=True)
        oi = jnp.dot(p.astype(BF16), v, preferred_element_type=F32)
        l = li if l is None else l + li
        o = oi if o is None else o + oi
    return o / l


def _causal(s, q0):
    r = lax.broadcasted_iota(jnp.int32, s.shape, 0) + q0
    c = lax.broadcasted_iota(jnp.int32, s.shape, 1)
    return jnp.where(c <= r, s, NEG_INF)


def _prompt_attn_kernel(q_ref, k_ref, v_ref, qm_ref, km_ref, vm_ref, fr_ref, fm_ref, o_ref, om_ref):
    seq = q_ref.shape[1]
    kb = k_ref[0].astype(BF16)
    vb = v_ref[0].astype(BF16)
    kmb = km_ref[0].astype(BF16)
    vmb = vm_ref[0].astype(BF16)
    f_real = fr_ref[0]
    f_meta = fm_ref[0][:, :N_META]

    qmb = (qm_ref[0] * ATTN_SCALE).astype(BF16)
    sm = _causal(_dot_nt(qmb, kmb) - f_meta, 0)
    om_ref[...] = _softmax_pv([(sm, vmb)])

    for qi in range(seq // Q_TILE):
        n = (qi + 1) * Q_TILE
        qb = (q_ref[0, qi * Q_TILE:n, :] * ATTN_SCALE).astype(BF16)
        s_real = _causal(_dot_nt(qb, kb[:n]) - f_real[:, :n], qi * Q_TILE)
        s_meta = _dot_nt(qb, kmb) - f_meta
        o_ref[qi * Q_TILE:n, :] = _softmax_pv([(s_real, vb[:n]), (s_meta, vmb)])


def _prompt_attention(z4, f_real, f_meta, batch, seq, n_heads, meta_row0):
    meta_blk0 = meta_row0 // N_META

    def zspec(which, meta):
        if meta:
            return pl.BlockSpec((1, N_META, HEAD_DIM), lambda b, h: (which, meta_blk0 + b, h))
        return pl.BlockSpec((1, seq, HEAD_DIM), lambda b, h: (which, b, h))

    return pl.pallas_call(
        _prompt_attn_kernel,
        grid=(batch, n_heads),
        in_specs=[
            zspec(0, False), zspec(1, False), zspec(2, False),
            zspec(0, True), zspec(1, True), zspec(2, True),
            pl.BlockSpec((1, 1, seq), lambda b, h: (b * n_heads + h, 0, 0)),
            pl.BlockSpec((1, 1, LANE), lambda b, h: (b * n_heads + h, 0, 0)),
        ],
        out_specs=[
            pl.BlockSpec((seq, HEAD_DIM), lambda b, h: (b, h)),
            pl.BlockSpec((N_META, HEAD_DIM), lambda b, h: (b, h)),
        ],
        out_shape=[
            jax.ShapeDtypeStruct((batch * seq, n_heads * HEAD_DIM), F32),
            jax.ShapeDtypeStruct((batch * N_META, n_heads * HEAD_DIM), F32),
        ],
        compiler_params=pltpu.CompilerParams(
            dimension_semantics=("parallel", "parallel"), vmem_limit_bytes=VMEM_LIMIT),
        name="prompt_attn",
    )(z4, z4, z4, z4, z4, z4, f_real, f_meta)


def _decode_kernel(pt_ref, q_ref, kn_ref, vn_ref, lfn_ref, kc_ref, vc_ref, lfc_ref, tri_ref,
                   o_ref, m_ref, l_ref, carry_ref, acc_ref):
    del pt_ref
    j = pl.program_id(1)
    n_heads = acc_ref.shape[0]
    width = acc_ref.shape[1]

    @pl.when(j == 0)
    def _():
        m_ref[...] = jnp.full(m_ref.shape, -jnp.inf, F32)
        l_ref[...] = jnp.zeros(l_ref.shape, F32)
        carry_ref[...] = jnp.zeros(carry_ref.shape, F32)
        acc_ref[...] = jnp.zeros(acc_ref.shape, F32)

    row = lax.broadcasted_iota(jnp.int32, (n_heads, width), 0)
    col = lax.broadcasted_iota(jnp.int32, (n_heads, width), 1)
    head_block = (col // HEAD_DIM) == row
    head_inter = (col % n_heads) == row

    q_row = q_ref[0] * ATTN_SCALE
    q_bd = jnp.where(head_block, jnp.broadcast_to(q_row, (n_heads, width)), 0.0)
    s = _dot_nt(q_bd.astype(BF16), kc_ref[0].astype(BF16))

    lf_e = jnp.where(head_inter, jnp.broadcast_to(lfc_ref[0], (n_heads, width)), 0.0)
    c = _dot3(lf_e, tri_ref[...])
    f = carry_ref[...] + c
    logit = s - f

    m_old = m_ref[...]
    m_new = jnp.maximum(m_old, jnp.max(logit, axis=-1, keepdims=True))
    alpha = jnp.exp(m_old - m_new)
    p = jnp.exp(logit - m_new)
    l_ref[...] = alpha * l_ref[...] + jnp.sum(p, axis=-1, keepdims=True)
    acc_ref[...] = alpha * acc_ref[...] + jnp.dot(
        p.astype(BF16), vc_ref[0].astype(BF16), preferred_element_type=F32)
    m_ref[...] = m_new
    carry_ref[...] = f[:, PAGE_SIZE - 1:PAGE_SIZE]

    @pl.when(j == pl.num_programs(1) - 1)
    def _():
        s_new = jnp.sum(jnp.where(head_block, jnp.broadcast_to(q_row * kn_ref[0], (n_heads, width)), 0.0),
                        axis=-1, keepdims=True)
        r8 = lax.broadcasted_iota(jnp.int32, (n_heads, n_heads), 0)
        c8 = lax.broadcasted_iota(jnp.int32, (n_heads, n_heads), 1)
        lf_new = jnp.sum(jnp.where(r8 == c8, jnp.broadcast_to(lfn_ref[0], (n_heads, n_heads)), 0.0),
                         axis=-1, keepdims=True)
        logit_new = s_new - (carry_ref[...] + lf_new)
        m_old2 = m_ref[...]
        m_fin = jnp.maximum(m_old2, logit_new)
        a2 = jnp.exp(m_old2 - m_fin)
        p_new = jnp.exp(logit_new - m_fin)
        l_fin = a2 * l_ref[...] + p_new
        acc = a2 * acc_ref[...] + p_new * jnp.broadcast_to(vn_ref[0], (n_heads, width))
        o_ref[0] = jnp.sum(jnp.where(head_block, acc / l_fin, 0.0), axis=0, keepdims=True)


def _decode_attention(page_table, q, k_new, v_new, lf_new, cache_k, cache_v, cache_lf, tri, layer, n_phys):
    n_seq, n_pages = page_table.shape
    width = q.shape[-1]
    n_heads = width // HEAD_DIM

    def page_map(b, j, pt):
        return (layer * n_phys + pt[b * n_pages + j], 0, 0)

    def row_map(b, j, pt):
        return (b, 0, 0)

    grid_spec = pltpu.PrefetchScalarGridSpec(
        num_scalar_prefetch=1,
        grid=(n_seq, n_pages),
        in_specs=[
            pl.BlockSpec((1, 1, width), row_map),
            pl.BlockSpec((1, 1, width), row_map),
            pl.BlockSpec((1, 1, width), row_map),
            pl.BlockSpec((1, 1, n_heads), row_map),
            pl.BlockSpec((1, PAGE_SIZE, width), page_map),
            pl.BlockSpec((1, PAGE_SIZE, width), page_map),
            pl.BlockSpec((1, 1, PAGE_SIZE * n_heads), page_map),
            pl.BlockSpec((PAGE_SIZE * n_heads, PAGE_SIZE), lambda b, j, pt: (0, 0)),
        ],
        out_specs=pl.BlockSpec((1, 1, width), row_map),
        scratch_shapes=[
            pltpu.VMEM((n_heads, 1), F32),
            pltpu.VMEM((n_heads, 1), F32),
            pltpu.VMEM((n_heads, 1), F32),
            pltpu.VMEM((n_heads, width), F32),
        ],
    )
    return pl.pallas_call(
        _decode_kernel,
        grid_spec=grid_spec,
        out_shape=jax.ShapeDtypeStruct((n_seq, 1, width), F32),
        compiler_params=pltpu.CompilerParams(
            dimension_semantics=("parallel", "arbitrary"), vmem_limit_bytes=VMEM_LIMIT),
        name="decode_attn",
    )(page_table.reshape(-1), q, k_new, v_new, lf_new, cache_k, cache_v, cache_lf, tri)


def _pool_project(d_groups, wp_ref, sc_ref, o_ref):
    for g, d in enumerate(d_groups):
        gw = d.shape[1]
        cs = slice(g * gw, (g + 1) * gw)
        o_ref[:, cs] = jnp.dot(d.astype(BF16), wp_ref[g], preferred_element_type=F32) * sc_ref[:, cs]


def _pool_real_kernel(halo_ref, cur_ref, wp_ref, sc_ref, o_ref, ext_ref):
    rows = cur_ref.shape[0]
    gw = cur_ref.shape[1] // len(POOL_WINDOWS)
    ext_ref[0:N_META, :] = halo_ref[...]
    ext_ref[N_META:, :] = cur_ref[...]
    ds = []
    for g, w in enumerate(POOL_WINDOWS):
        cs = slice(g * gw, (g + 1) * gw)
        cur = ext_ref[N_META:N_META + rows, cs]
        s = cur
        for back in range(1, w):
            s = s + ext_ref[N_META - back:N_META - back + rows, cs]
        ds.append(s * (1.0 / w) - cur)
    _pool_project(ds, wp_ref, sc_ref, o_ref)


def _pool_real(p_all, w_pool, scale, batch, seq, meta_row0):
    width = p_all.shape[1]
    tiles = seq // POOL_TILE
    per16 = POOL_TILE // N_META
    meta_blk0 = meta_row0 // N_META

    def halo_map(b, i):
        return (jnp.where(i == 0, meta_blk0 + b, (b * tiles + i) * per16 - 1), 0)

    return pl.pallas_call(
        _pool_real_kernel,
        grid=(batch, tiles),
        in_specs=[
            pl.BlockSpec((N_META, width), halo_map),
            pl.BlockSpec((POOL_TILE, width), lambda b, i: (b * tiles + i, 0)),
            pl.BlockSpec(w_pool.shape, lambda b, i: (0, 0, 0)),
            pl.BlockSpec((1, width), lambda b, i: (0, 0)),
        ],
        out_specs=pl.BlockSpec((POOL_TILE, width), lambda b, i: (b * tiles + i, 0)),
        out_shape=jax.ShapeDtypeStruct((batch * seq, width), F32),
        scratch_shapes=[pltpu.VMEM((N_META + POOL_TILE, width), F32)],
        compiler_params=pltpu.CompilerParams(
            dimension_semantics=("parallel", "parallel"), vmem_limit_bytes=VMEM_LIMIT),
        name="pool_real",
    )(p_all, p_all, w_pool, scale)


def _pool_small_kernel(pm_ref, ps_ref, st_ref, wp_ref, sc_ref, om_ref, os_ref, ext_ref):
    width = pm_ref.shape[1]
    gw = width // len(POOL_WINDOWS)
    ext_ref[0:N_META, :] = jnp.zeros((N_META, width), F32)
    pos = lax.broadcasted_iota(jnp.int32, (N_META, 1), 0)
    for b in range(pm_ref.shape[0] // N_META):
        ext_ref[N_META:, :] = pm_ref[b * N_META:(b + 1) * N_META, :]
        ds = []
        for g, w in enumerate(POOL_WINDOWS):
            cs = slice(g * gw, (g + 1) * gw)
            cur = ext_ref[N_META:2 * N_META, cs]
            s = cur
            for back in range(1, w):
                s = s + ext_ref[N_META - back:2 * N_META - back, cs]
            cnt = jnp.minimum(pos + 1, w).astype(F32)
            ds.append(s / cnt - cur)
        _pool_project(ds, wp_ref, sc_ref, om_ref.at[b * N_META:(b + 1) * N_META, :])
    ds = []
    for g, w in enumerate(POOL_WINDOWS):
        cs = slice(g * gw, (g + 1) * gw)
        cur = ps_ref[:, cs]
        s = cur
        for back in range(1, w):
            s = s + st_ref[POOL_CTX - back, :, cs]
        ds.append(s * (1.0 / w) - cur)
    _pool_project(ds, wp_ref, sc_ref, os_ref)


def _pool_small(p_meta, p_sample, state_t, w_pool, scale):
    return pl.pallas_call(
        _pool_small_kernel,
        out_shape=[jax.ShapeDtypeStruct(p_meta.shape, F32), jax.ShapeDtypeStruct(p_sample.shape, F32)],
        scratch_shapes=[pltpu.VMEM((2 * N_META, p_meta.shape[1]), F32)],
        compiler_params=pltpu.CompilerParams(vmem_limit_bytes=VMEM_LIMIT),
        name="pool_small",
    )(p_meta, p_sample, state_t, w_pool, scale)


def _outproj_kernel(h_ref, a_ref, pm_ref, w_ref, g_ref, o_ref):
    aw = a_ref.shape[1]
    mix = jnp.dot(a_ref[...].astype(BF16), w_ref[0:aw, :], preferred_element_type=F32)
    mix = mix + jnp.dot(pm_ref[...].astype(BF16), w_ref[aw:, :], preferred_element_type=F32)
    o_ref[...] = h_ref[...] + _rms(mix, g_ref[...])


def _out_proj(h, a, pm, w_out, g):
    rows, d = h.shape
    return pl.pallas_call(
        _outproj_kernel,
        grid=(rows // ROW_TILE,),
        in_specs=[
            pl.BlockSpec((ROW_TILE, d), lambda i: (i, 0)),
            pl.BlockSpec((ROW_TILE, a.shape[1]), lambda i: (i, 0)),
            pl.BlockSpec((ROW_TILE, pm.shape[1]), lambda i: (i, 0)),
            pl.BlockSpec(w_out.shape, lambda i: (0, 0)),
            pl.BlockSpec((1, d), lambda i: (0, 0)),
        ],
        out_specs=pl.BlockSpec((ROW_TILE, d), lambda i: (i, 0)),
        out_shape=jax.ShapeDtypeStruct((rows, d), F32),
        compiler_params=pltpu.CompilerParams(
            dimension_semantics=("parallel",), vmem_limit_bytes=VMEM_LIMIT),
        name="out_proj",
    )(h, a, pm, w_out, g)


def _pad_ff(w, axis, ff_pad):
    pad = [(0, 0), (0, 0)]
    pad[axis] = (0, ff_pad - w.shape[axis])
    return jnp.pad(w.astype(BF16), pad)


def kernel(x_prompt, x_sample, cache_k, cache_v, cache_logf, state_pool, page_table, meta_tokens, w_in, b_forget, w_pool, pool_scale, w_out, ffn1_gate, ffn1_up, ffn1_down, ffn2_gate, ffn2_up, ffn2_down, g_pre_ffn1, g_post_ffn1, g_pre_mix, g_post_mix, g_pre_ffn2, g_post_ffn2):
    batch, seq, d_model = x_prompt.shape
    n_seq = x_sample.shape[0]
    depth, n_phys, page, n_heads, head_dim = cache_k.shape
    attn_w = n_heads * head_dim
    pool_w = state_pool.shape[-1]
    d_ff = ffn1_gate.shape[-1]
    ff_pad = -(-d_ff // FF_TILE) * FF_TILE
    assert head_dim == HEAD_DIM and page == PAGE_SIZE and x_sample.shape[1] == 1
    assert attn_w == pool_w and w_in.shape[-1] == 3 * attn_w + pool_w + n_heads

    n_real = batch * seq
    s0 = n_real
    m0 = s0 + n_seq
    n_used = m0 + batch * N_META
    assert n_used <= ROW_ALIGN and m0 % N_META == 0 and seq % POOL_TILE == 0 and seq % Q_TILE == 0

    h = jnp.concatenate([
        x_prompt.reshape(n_real, d_model),
        x_sample.reshape(n_seq, d_model),
        jnp.tile(meta_tokens.astype(F32), (batch, 1)),
        jnp.zeros((ROW_ALIGN - n_used, d_model), F32)], axis=0)

    ck = cache_k.reshape(depth * n_phys, page, attn_w)
    cv = cache_v.reshape(depth * n_phys, page, attn_w)
    clf = cache_logf.reshape(depth * n_phys, 1, page * n_heads)
    lane = jnp.arange(page * n_heads, dtype=jnp.int32)
    tri = (lane[:, None] // n_heads <= jnp.arange(page, dtype=jnp.int32)[None, :]).astype(BF16)
    pad_rows = jnp.zeros((ROW_ALIGN - n_used, attn_w), F32)

    def row(v):
        return v.reshape(1, -1)

    outs = {k: [] for k in ("kp", "vp", "lfp", "pp", "ks", "vs", "lfs", "ps")}
    for l in range(depth):
        ffn1 = (_pad_ff(ffn1_gate[l], 1, ff_pad), _pad_ff(ffn1_up[l], 1, ff_pad), _pad_ff(ffn1_down[l], 0, ff_pad))
        ffn2 = (_pad_ff(ffn2_gate[l], 1, ff_pad), _pad_ff(ffn2_up[l], 1, ff_pad), _pad_ff(ffn2_down[l], 0, ff_pad))
        w_main = w_in[l][:, :3 * attn_w + pool_w].astype(BF16)
        w_f = jnp.pad(w_in[l][:, 3 * attn_w + pool_w:].astype(BF16), ((0, 0), (0, LANE - n_heads)))
        b_f = jnp.pad(b_forget[l].astype(F32), (0, LANE - n_heads)).reshape(1, LANE)

        h = _ffn_half(h, row(g_pre_ffn1[l]), row(g_post_ffn1[l]), *ffn1)

        z4, lf = _in_proj(h, row(g_pre_mix[l]), w_main, w_f, b_f, n_heads)

        lf_real_t = lf[:n_real].reshape(batch, seq, n_heads).transpose(0, 2, 1).reshape(batch * n_heads, seq)
        lf_meta_t = lf[m0:n_used].reshape(batch, N_META, n_heads).transpose(0, 2, 1).reshape(batch * n_heads, N_META)
        lf_meta_t = jnp.pad(lf_meta_t, ((0, 0), (0, LANE - N_META)))
        f_meta, f_real = _prompt_fcum(lf_meta_t, lf_real_t)
        a_real, a_meta = _prompt_attention(
            z4, f_real.reshape(batch * n_heads, 1, seq), f_meta.reshape(batch * n_heads, 1, LANE),
            batch, seq, n_heads, m0)

        q_s = z4[0, s0:m0].reshape(n_seq, 1, attn_w)
        k_s = z4[1, s0:m0].reshape(n_seq, 1, attn_w)
        v_s = z4[2, s0:m0].reshape(n_seq, 1, attn_w)
        p_s = z4[3, s0:m0]
        lf_s = lf[s0:m0]
        a_s = _decode_attention(page_table, q_s, k_s, v_s, lf_s.reshape(n_seq, 1, n_heads),
                                ck, cv, clf, tri, l, n_phys).reshape(n_seq, attn_w)

        wp = w_pool[l].astype(BF16)
        sc = row(pool_scale[l].astype(F32))
        pm_real = _pool_real(z4[3], wp, sc, batch, seq, m0)
        pm_meta, pm_s = _pool_small(z4[3, m0:n_used], p_s, state_pool[l].transpose(1, 0, 2), wp, sc)

        a_all = jnp.concatenate([a_real, a_s, a_meta, pad_rows], axis=0)
        pm_all = jnp.concatenate([pm_real, pm_s, pm_meta, pad_rows], axis=0)
        h = _out_proj(h, a_all, pm_all, w_out[l].astype(BF16), row(g_post_mix[l]))

        h = _ffn_half(h, row(g_pre_ffn2[l]), row(g_post_ffn2[l]), *ffn2)

        def seq_major(x_real, x_meta, tail):
            return jnp.concatenate(
                [x_meta.reshape((batch, N_META) + tail), x_real.reshape((batch, seq) + tail)], axis=1)

        outs["kp"].append(seq_major(z4[1, :n_real], z4[1, m0:n_used], (n_heads, head_dim)))
        outs["vp"].append(seq_major(z4[2, :n_real], z4[2, m0:n_used], (n_heads, head_dim)))
        outs["lfp"].append(seq_major(lf[:n_real], lf[m0:n_used], (n_heads,)))
        outs["pp"].append(z4[3, :n_real].reshape(batch, seq, pool_w)[:, seq - POOL_CTX:])
        outs["ks"].append(k_s.reshape(n_seq, 1, n_heads, head_dim))
        outs["vs"].append(v_s.reshape(n_seq, 1, n_heads, head_dim))
        outs["lfs"].append(lf_s.reshape(n_seq, 1, n_heads))
        outs["ps"].append(jnp.concatenate([state_pool[l][:, 1:], p_s[:, None, :]], axis=1))

    y_prompt = h[:n_real].reshape(batch, seq, d_model)
    y_sample = h[s0:m0].reshape(n_seq, 1, d_model)
    st = {k: jnp.stack(v, 0) for k, v in outs.items()}
    return (y_prompt, y_sample, st["kp"], st["vp"], st["lfp"], st["pp"],
            st["ks"], st["vs"], st["lfs"], st["ps"])
```

```python
import functools

import jax
import jax.numpy as jnp
from jax import lax
from jax.experimental import pallas as pl
from jax.experimental.pallas import tpu as pltpu

F32 = jnp.float32
BF16 = jnp.bfloat16

N_META = 16
HEAD_DIM = 128
PAGE_SIZE = 128
POOL_WINDOWS = (2, 4, 8, 16)
POOL_CTX = max(POOL_WINDOWS) - 1
RMS_EPS = 1e-6
NEG_INF = -1e30
ATTN_SCALE = HEAD_DIM ** -0.5

LANE = 128
SUBLANE = 8
ROW_TILE = 768
ROW_ALIGN = 8448
FF_TILE = 256
POOL_TILE = 512
Q_TILE = 256
VMEM_LIMIT = 56 * 1024 * 1024


def _rms(x, g):
    return x * lax.rsqrt(jnp.mean(x * x, axis=-1, keepdims=True) + RMS_EPS) * g


def _split3(x):
    hi = x.astype(BF16)
    r1 = x - hi.astype(F32)
    mid = r1.astype(BF16)
    lo = (r1 - mid.astype(F32)).astype(BF16)
    return hi, mid, lo


def _dot3(x, w01):
    hi, mid, lo = _split3(x)
    acc = jnp.dot(hi, w01, preferred_element_type=F32)
    acc = acc + jnp.dot(mid, w01, preferred_element_type=F32)
    return acc + jnp.dot(lo, w01, preferred_element_type=F32)


def _dot3_rhs(w01, x):
    hi, mid, lo = _split3(x)
    acc = jnp.dot(w01, hi, preferred_element_type=F32)
    acc = acc + jnp.dot(w01, mid, preferred_element_type=F32)
    return acc + jnp.dot(w01, lo, preferred_element_type=F32)


def _dot_nt(a, b):
    return lax.dot_general(a, b, (((1,), (1,)), ((), ())), preferred_element_type=F32)


def _dot_mixed(a, b):
    return lax.dot_general(a, b, (((1,), (0,)), ((), ())), preferred_element_type=F32)


def _ffn_kernel(x_ref, gpre_ref, gpost_ref, wg_ref, wu_ref, wd_ref, wgt_ref, wut_ref, wdt_ref, o_ref, xn_ref):
    j = pl.program_id(1)

    @pl.when(j == 0)
    def _():
        xn_ref[...] = _rms(x_ref[...], gpre_ref[...]).astype(BF16)
        o_ref[...] = jnp.zeros(o_ref.shape, F32)

    def swiglu_part(wg, wu, wd):
        xn = xn_ref[...]
        a = _dot_mixed(xn, wg)
        b = _dot_mixed(xn, wu)
        return _dot_mixed((a * jax.nn.sigmoid(a) * b).astype(BF16), wd)

    o_ref[...] += swiglu_part(wg_ref[...], wu_ref[...], wd_ref[...])

    @pl.when(j == pl.num_programs(1) - 1)
    def _():
        y = o_ref[...] + swiglu_part(wgt_ref[...], wut_ref[...], wdt_ref[...])
        o_ref[...] = x_ref[...] + 0.5 * _rms(y, gpost_ref[...])


def _ffn_half(h, g_pre, g_post, wg, wu, wd, layer):
    rows, d = h.shape
    ff = wg.shape[2]
    n_main = ff // FF_TILE
    tail = ff - n_main * FF_TILE
    assert tail > 0 and tail % LANE == 0
    wg_t = wg[layer, :, n_main * FF_TILE:]
    wu_t = wu[layer, :, n_main * FF_TILE:]
    wd_t = wd[layer, n_main * FF_TILE:, :]
    return pl.pallas_call(
        _ffn_kernel,
        grid=(rows // ROW_TILE, n_main),
        in_specs=[
            pl.BlockSpec((ROW_TILE, d), lambda i, j: (i, 0)),
            pl.BlockSpec((1, d), lambda i, j: (0, 0)),
            pl.BlockSpec((1, d), lambda i, j: (0, 0)),
            pl.BlockSpec((None, d, FF_TILE), lambda i, j: (layer, 0, j)),
            pl.BlockSpec((None, d, FF_TILE), lambda i, j: (layer, 0, j)),
            pl.BlockSpec((None, FF_TILE, d), lambda i, j: (layer, j, 0)),
            pl.BlockSpec((d, tail), lambda i, j: (0, 0)),
            pl.BlockSpec((d, tail), lambda i, j: (0, 0)),
            pl.BlockSpec((tail, d), lambda i, j: (0, 0)),
        ],
        out_specs=pl.BlockSpec((ROW_TILE, d), lambda i, j: (i, 0)),
        out_shape=jax.ShapeDtypeStruct((rows, d), F32),
        scratch_shapes=[pltpu.VMEM((ROW_TILE, d), BF16)],
        compiler_params=pltpu.CompilerParams(
            dimension_semantics=("parallel", "arbitrary"), vmem_limit_bytes=VMEM_LIMIT),
        name="ffn_half",
    )(h, g_pre, g_post, wg, wu, wd, wg_t, wu_t, wd_t)


def _inproj_kernel(h_ref, g_ref, w_ref, wf_ref, bf_ref, z_ref, lf_ref, u_ref):
    n = pl.program_id(1)

    @pl.when(n == 0)
    def _():
        u = _rms(h_ref[...], g_ref[...]).astype(BF16)
        u_ref[...] = u
        x = _dot_mixed(u, wf_ref[...]) + bf_ref[...]
        lf = jnp.minimum(x, 0.0) - jnp.log1p(jnp.exp(-jnp.abs(x)))
        lf_ref[...] = lf[:, :lf_ref.shape[1]]

    z_ref[0] = _dot_mixed(u_ref[...], w_ref[...])


def _in_proj(h, g, w_in, w_f, b_f, n_heads, layer):
    rows, d = h.shape
    col_tile = (w_in.shape[2] - n_heads) // 4
    return pl.pallas_call(
        _inproj_kernel,
        grid=(rows // ROW_TILE, 4),
        in_specs=[
            pl.BlockSpec((ROW_TILE, d), lambda i, n: (i, 0)),
            pl.BlockSpec((1, d), lambda i, n: (0, 0)),
            pl.BlockSpec((None, d, col_tile), lambda i, n: (layer, 0, n)),
            pl.BlockSpec((d, LANE), lambda i, n: (0, 0)),
            pl.BlockSpec((1, LANE), lambda i, n: (0, 0)),
        ],
        out_specs=[
            pl.BlockSpec((1, ROW_TILE, col_tile), lambda i, n: (n, i, 0)),
            pl.BlockSpec((ROW_TILE, n_heads), lambda i, n: (i, 0)),
        ],
        out_shape=[
            jax.ShapeDtypeStruct((4, rows, col_tile), F32),
            jax.ShapeDtypeStruct((rows, n_heads), F32),
        ],
        scratch_shapes=[pltpu.VMEM((ROW_TILE, d), BF16)],
        compiler_params=pltpu.CompilerParams(
            dimension_semantics=("parallel", "arbitrary"), vmem_limit_bytes=VMEM_LIMIT),
        name="in_proj",
    )(h, g, w_in, w_f, b_f)


def _ones_where(shape, pred):
    r = lax.broadcasted_iota(jnp.int32, shape, 0)
    c = lax.broadcasted_iota(jnp.int32, shape, 1)
    return jnp.where(pred(r, c), 1.0, 0.0).astype(BF16)


def _fcum_kernel(lfm_ref, lfr_ref, fm_ref, fr_ref):
    u = _ones_where((LANE, LANE), lambda r, c: r <= c)
    fm = _dot3(lfm_ref[...], u)
    fm_ref[...] = fm
    carry = fm[:, LANE - 1:LANE]
    for blk in range(lfr_ref.shape[1] // LANE):
        sl = slice(blk * LANE, (blk + 1) * LANE)
        f = _dot3(lfr_ref[:, sl], u) + carry
        fr_ref[:, sl] = f
        carry = f[:, LANE - 1:LANE]


def _prompt_fcum(lf_meta_t, lf_real_t):
    return pl.pallas_call(
        _fcum_kernel,
        out_shape=[jax.ShapeDtypeStruct(lf_meta_t.shape, F32), jax.ShapeDtypeStruct(lf_real_t.shape, F32)],
        name="prompt_fcum",
    )(lf_meta_t, lf_real_t)


def _softmax_pv(parts):
    m = None
    for s, _ in parts:
        mi = jnp.max(s, axis=-1, keepdims=True)
        m = mi if m is None else jnp.maximum(m, mi)
    l = None
    o = None
    for s, v in parts:
        p = jnp.exp(s - m)
        li = jnp.sum(p, axis=-1, keepdims=True)
        oi = jnp.dot(p.astype(BF16), v, preferred_element_type=F32)
        l = li if l is None else l + li
        o = oi if o is None else o + oi
    return o / l


def _causal(s, q0):
    r = lax.broadcasted_iota(jnp.int32, s.shape, 0) + q0
    c = lax.broadcasted_iota(jnp.int32, s.shape, 1)
    return jnp.where(c <= r, s, NEG_INF)


def _prompt_attn_kernel(q_ref, k_ref, v_ref, qm_ref, km_ref, vm_ref, fr_ref, fm_ref, o_ref, om_ref):
    seq = q_ref.shape[1]
    kb = k_ref[0].astype(BF16)
    vb = v_ref[0].astype(BF16)
    kmb = km_ref[0].astype(BF16)
    vmb = vm_ref[0].astype(BF16)
    f_real = fr_ref[0]
    f_meta = fm_ref[0][:, :N_META]

    qmb = (qm_ref[0] * ATTN_SCALE).astype(BF16)
    sm = _causal(_dot_nt(qmb, kmb) - f_meta, 0)
    om_ref[...] = _softmax_pv([(sm, vmb)])

    for qi in range(seq // Q_TILE):
        n = (qi + 1) * Q_TILE
        qb = (q_ref[0, qi * Q_TILE:n, :] * ATTN_SCALE).astype(BF16)
        s_real = _causal(_dot_nt(qb, kb[:n]) - f_real[:, :n], qi * Q_TILE)
        s_meta = _dot_nt(qb, kmb) - f_meta
        o_ref[qi * Q_TILE:n, :] = _softmax_pv([(s_real, vb[:n]), (s_meta, vmb)])


def _prompt_attention(z4, f_real, f_meta, batch, seq, n_heads, meta_row0):
    meta_blk0 = meta_row0 // N_META

    def zspec(which, meta):
        if meta:
            return pl.BlockSpec((1, N_META, HEAD_DIM), lambda b, h: (which, meta_blk0 + b, h))
        return pl.BlockSpec((1, seq, HEAD_DIM), lambda b, h: (which, b, h))

    return pl.pallas_call(
        _prompt_attn_kernel,
        grid=(batch, n_heads),
        in_specs=[
            zspec(0, False), zspec(1, False), zspec(2, False),
            zspec(0, True), zspec(1, True), zspec(2, True),
            pl.BlockSpec((1, 1, seq), lambda b, h: (b * n_heads + h, 0, 0)),
            pl.BlockSpec((1, 1, LANE), lambda b, h: (b * n_heads + h, 0, 0)),
        ],
        out_specs=[
            pl.BlockSpec((seq, HEAD_DIM), lambda b, h: (b, h)),
            pl.BlockSpec((N_META, HEAD_DIM), lambda b, h: (b, h)),
        ],
        out_shape=[
            jax.ShapeDtypeStruct((batch * seq, n_heads * HEAD_DIM), F32),
            jax.ShapeDtypeStruct((batch * N_META, n_heads * HEAD_DIM), F32),
        ],
        compiler_params=pltpu.CompilerParams(
            dimension_semantics=("parallel", "parallel"), vmem_limit_bytes=VMEM_LIMIT),
        name="prompt_attn",
    )(z4, z4, z4, z4, z4, z4, f_real, f_meta)


def _decode_kernel(pt_ref, q_ref, kn_ref, vn_ref, lfn_ref, ck_ref, cv_ref, clf_ref, o_ref,
                   kbuf, vbuf, lfbuf, f_ref, sems, *, layer, n_pages):
    b = pl.program_id(0)
    nb = pl.num_programs(0)
    half = n_pages // 2
    n_heads = q_ref.shape[1]
    keys_half = half * PAGE_SIZE * n_heads
    n_chunks = keys_half // LANE
    slot = b % 2

    def copies(seq, slot_):
        out = []
        for j in range(n_pages):
            page = pt_ref[seq * n_pages + j]
            dst_lanes = pl.ds((j // half) * HEAD_DIM, HEAD_DIM)
            out.append(pltpu.make_async_copy(
                ck_ref.at[layer, page], kbuf.at[slot_, j % half, :, :, dst_lanes], sems.at[0, slot_]))
            out.append(pltpu.make_async_copy(
                cv_ref.at[layer, page], vbuf.at[slot_, j % half, :, :, dst_lanes], sems.at[1, slot_]))
            out.append(pltpu.make_async_copy(
                clf_ref.at[layer, page], lfbuf.at[slot_, j], sems.at[2, slot_]))
        return out

    @pl.when(b == 0)
    def _():
        for c in copies(0, 0):
            c.start()

    @pl.when(b + 1 < nb)
    def _():
        for c in copies(b + 1, 1 - slot):
            c.start()

    for c in copies(b, slot):
        c.wait()

    x = lfbuf[slot].reshape(n_pages * SUBLANE, LANE)
    nr = x.shape[0]
    w_in_row = _ones_where((LANE, LANE), lambda r, c: (r % n_heads == c % n_heads) & (r <= c))
    w_row_tot = _ones_where((LANE, LANE), lambda r, c: r % n_heads == c % n_heads)
    l_strict = _ones_where((nr, nr), lambda r, c: c < r)
    f_all = _dot3(x, w_in_row) + _dot3_rhs(l_strict, _dot3(x, w_row_tot))
    f_ref[...] = f_all

    q = q_ref[0] * ATTN_SCALE
    zero = jnp.zeros_like(q)
    q2 = jnp.concatenate([jnp.concatenate([q, zero], axis=1),
                          jnp.concatenate([zero, q], axis=1)], axis=0)
    kmat = kbuf[slot].reshape(keys_half, 2 * HEAD_DIM)
    s2 = _dot_nt(q2, kmat)

    f_lo = jnp.concatenate(
        [jnp.broadcast_to(f_ref[c:c + 1, :], (n_heads, LANE)) for c in range(n_chunks)], axis=1)
    f_hi = jnp.concatenate(
        [jnp.broadcast_to(f_ref[n_chunks + c:n_chunks + c + 1, :], (n_heads, LANE)) for c in range(n_chunks)], axis=1)
    f2 = jnp.concatenate([f_lo, f_hi], axis=0)

    row = lax.broadcasted_iota(jnp.int32, s2.shape, 0)
    col = lax.broadcasted_iota(jnp.int32, s2.shape, 1)
    logit = jnp.where(col % n_heads == row % n_heads, s2 - f2, -jnp.inf)
    m2 = jnp.max(logit, axis=-1, keepdims=True)
    m = jnp.maximum(m2[:n_heads], m2[n_heads:])
    p = jnp.exp(logit - jnp.concatenate([m, m], axis=0))
    l2 = jnp.sum(p, axis=-1, keepdims=True)
    l = l2[:n_heads] + l2[n_heads:]
    vmat = vbuf[slot].reshape(keys_half, 2 * HEAD_DIM)
    o2 = jnp.dot(p, vmat, preferred_element_type=F32)
    o = o2[:n_heads, :HEAD_DIM] + o2[n_heads:, HEAD_DIM:]

    last = jnp.broadcast_to(f_ref[nr - 1:nr, :], (n_heads, LANE))
    r8 = lax.broadcasted_iota(jnp.int32, (n_heads, LANE), 0)
    c8 = lax.broadcasted_iota(jnp.int32, (n_heads, LANE), 1)
    f_past = jnp.sum(jnp.where(c8 == LANE - n_heads + r8, last, 0.0), axis=-1, keepdims=True)
    s_new = jnp.sum(q * kn_ref[0], axis=-1, keepdims=True)
    logit_new = s_new - (f_past + lfn_ref[0])
    m_fin = jnp.maximum(m, logit_new)
    alpha = jnp.exp(m - m_fin)
    p_new = jnp.exp(logit_new - m_fin)
    o_ref[0] = (alpha * o + p_new * vn_ref[0]) / (alpha * l + p_new)


def _decode_attention(page_table, q, k_new, v_new, lf_new, cache_k, cache_v, cache_lf, layer):
    n_seq, n_pages = page_table.shape
    _, n_heads, head_dim = q.shape
    half = n_pages // 2
    assert n_pages % 2 == 0 and PAGE_SIZE * n_heads == SUBLANE * LANE and head_dim == HEAD_DIM

    def row_map(b, pt):
        return (b, 0, 0)

    any_spec = pl.BlockSpec(memory_space=pl.ANY)
    grid_spec = pltpu.PrefetchScalarGridSpec(
        num_scalar_prefetch=1,
        grid=(n_seq,),
        in_specs=[
            pl.BlockSpec((1, n_heads, head_dim), row_map),
            pl.BlockSpec((1, n_heads, head_dim), row_map),
            pl.BlockSpec((1, n_heads, head_dim), row_map),
            pl.BlockSpec((1, n_heads, 1), row_map),
            any_spec, any_spec, any_spec,
        ],
        out_specs=pl.BlockSpec((1, n_heads, head_dim), row_map),
        scratch_shapes=[
            pltpu.VMEM((2, half, PAGE_SIZE, n_heads, 2 * head_dim), F32),
            pltpu.VMEM((2, half, PAGE_SIZE, n_heads, 2 * head_dim), F32),
            pltpu.VMEM((2, n_pages, SUBLANE, LANE), F32),
            pltpu.VMEM((n_pages * SUBLANE, LANE), F32),
            pltpu.SemaphoreType.DMA((3, 2)),
        ],
    )
    return pl.pallas_call(
        functools.partial(_decode_kernel, layer=layer, n_pages=n_pages),
        grid_spec=grid_spec,
        out_shape=jax.ShapeDtypeStruct((n_seq, n_heads, head_dim), F32),
        compiler_params=pltpu.CompilerParams(
            dimension_semantics=("arbitrary",), vmem_limit_bytes=VMEM_LIMIT),
        name="decode_attn",
    )(page_table.reshape(-1), q, k_new, v_new, lf_new, cache_k, cache_v, cache_lf)


def _pool_project(d_groups, wp_ref, sc_ref, o_ref):
    for g, d in enumerate(d_groups):
        gw = d.shape[1]
        cs = slice(g * gw, (g + 1) * gw)
        o_ref[:, cs] = jnp.dot(d.astype(BF16), wp_ref[g], preferred_element_type=F32) * sc_ref[:, cs]


def _pool_real_kernel(halo_ref, cur_ref, wp_ref, sc_ref, o_ref, ext_ref):
    rows = cur_ref.shape[0]
    gw = cur_ref.shape[1] // len(POOL_WINDOWS)
    ext_ref[0:N_META, :] = halo_ref[...]
    ext_ref[N_META:, :] = cur_ref[...]
    ds = []
    for g, w in enumerate(POOL_WINDOWS):
        cs = slice(g * gw, (g + 1) * gw)
        cur = ext_ref[N_META:N_META + rows, cs]
        s = cur
        for back in range(1, w):
            s = s + ext_ref[N_META - back:N_META - back + rows, cs]
        ds.append(s * (1.0 / w) - cur)
    _pool_project(ds, wp_ref, sc_ref, o_ref)


def _pool_real(p_all, w_pool, scale, batch, seq, meta_row0):
    width = p_all.shape[1]
    tiles = seq // POOL_TILE
    per16 = POOL_TILE // N_META
    meta_blk0 = meta_row0 // N_META

    def halo_map(b, i):
        return (jnp.where(i == 0, meta_blk0 + b, (b * tiles + i) * per16 - 1), 0)

    return pl.pallas_call(
        _pool_real_kernel,
        grid=(batch, tiles),
        in_specs=[
            pl.BlockSpec((N_META, width), halo_map),
            pl.BlockSpec((POOL_TILE, width), lambda b, i: (b * tiles + i, 0)),
            pl.BlockSpec(w_pool.shape, lambda b, i: (0, 0, 0)),
            pl.BlockSpec((1, width), lambda b, i: (0, 0)),
        ],
        out_specs=pl.BlockSpec((POOL_TILE, width), lambda b, i: (b * tiles + i, 0)),
        out_shape=jax.ShapeDtypeStruct((batch * seq, width), F32),
        scratch_shapes=[pltpu.VMEM((N_META + POOL_TILE, width), F32)],
        compiler_params=pltpu.CompilerParams(
            dimension_semantics=("parallel", "parallel"), vmem_limit_bytes=VMEM_LIMIT),
        name="pool_real",
    )(p_all, p_all, w_pool, scale)


def _pool_small_kernel(pm_ref, ps_ref, st_ref, wp_ref, sc_ref, om_ref, os_ref, ext_ref):
    width = pm_ref.shape[1]
    gw = width // len(POOL_WINDOWS)
    ext_ref[0:N_META, :] = jnp.zeros((N_META, width), F32)
    pos = lax.broadcasted_iota(jnp.int32, (N_META, 1), 0)
    for b in range(pm_ref.shape[0] // N_META):
        ext_ref[N_META:, :] = pm_ref[b * N_META:(b + 1) * N_META, :]
        ds = []
        for g, w in enumerate(POOL_WINDOWS):
            cs = slice(g * gw, (g + 1) * gw)
            cur = ext_ref[N_META:2 * N_META, cs]
            s = cur
            for back in range(1, w):
                s = s + ext_ref[N_META - back:2 * N_META - back, cs]
            cnt = jnp.minimum(pos + 1, w).astype(F32)
            ds.append(s / cnt - cur)
        _pool_project(ds, wp_ref, sc_ref, om_ref.at[b * N_META:(b + 1) * N_META, :])
    ds = []
    for g, w in enumerate(POOL_WINDOWS):
        cs = slice(g * gw, (g + 1) * gw)
        cur = ps_ref[:, cs]
        s = cur
        for back in range(1, w):
            s = s + st_ref[POOL_CTX - back, :, cs]
        ds.append(s * (1.0 / w) - cur)
    _pool_project(ds, wp_ref, sc_ref, os_ref)


def _pool_small(p_meta, p_sample, state_t, w_pool, scale):
    return pl.pallas_call(
        _pool_small_kernel,
        out_shape=[jax.ShapeDtypeStruct(p_meta.shape, F32), jax.ShapeDtypeStruct(p_sample.shape, F32)],
        scratch_shapes=[pltpu.VMEM((2 * N_META, p_meta.shape[1]), F32)],
        compiler_params=pltpu.CompilerParams(vmem_limit_bytes=VMEM_LIMIT),
        name="pool_small",
    )(p_meta, p_sample, state_t, w_pool, scale)


def _outproj_kernel(h_ref, a_ref, pm_ref, w_ref, g_ref, o_ref):
    aw = a_ref.shape[1]
    mix = jnp.dot(a_ref[...].astype(BF16), w_ref[0:aw, :], preferred_element_type=F32)
    mix = mix + jnp.dot(pm_ref[...].astype(BF16), w_ref[aw:, :], preferred_element_type=F32)
    o_ref[...] = h_ref[...] + _rms(mix, g_ref[...])


def _out_proj(h, a, pm, w_out, g):
    rows, d = h.shape
    return pl.pallas_call(
        _outproj_kernel,
        grid=(rows // ROW_TILE,),
        in_specs=[
            pl.BlockSpec((ROW_TILE, d), lambda i: (i, 0)),
            pl.BlockSpec((ROW_TILE, a.shape[1]), lambda i: (i, 0)),
            pl.BlockSpec((ROW_TILE, pm.shape[1]), lambda i: (i, 0)),
            pl.BlockSpec(w_out.shape, lambda i: (0, 0)),
            pl.BlockSpec((1, d), lambda i: (0, 0)),
        ],
        out_specs=pl.BlockSpec((ROW_TILE, d), lambda i: (i, 0)),
        out_shape=jax.ShapeDtypeStruct((rows, d), F32),
        compiler_params=pltpu.CompilerParams(
            dimension_semantics=("parallel",), vmem_limit_bytes=VMEM_LIMIT),
        name="out_proj",
    )(h, a, pm, w_out, g)


def kernel(x_prompt, x_sample, cache_k, cache_v, cache_logf, state_pool, page_table, meta_tokens, w_in, b_forget, w_pool, pool_scale, w_out, ffn1_gate, ffn1_up, ffn1_down, ffn2_gate, ffn2_up, ffn2_down, g_pre_ffn1, g_post_ffn1, g_pre_mix, g_post_mix, g_pre_ffn2, g_post_ffn2):
    batch, seq, d_model = x_prompt.shape
    n_seq = x_sample.shape[0]
    depth, n_phys, page, n_heads, head_dim = cache_k.shape
    attn_w = n_heads * head_dim
    pool_w = state_pool.shape[-1]
    assert head_dim == HEAD_DIM and page == PAGE_SIZE and x_sample.shape[1] == 1
    assert attn_w == pool_w and w_in.shape[-1] == 3 * attn_w + pool_w + n_heads

    n_real = batch * seq
    s0 = n_real
    m0 = s0 + n_seq
    n_used = m0 + batch * N_META
    assert n_used <= ROW_ALIGN and m0 % N_META == 0 and seq % POOL_TILE == 0 and seq % Q_TILE == 0

    h = jnp.concatenate([
        x_prompt.reshape(n_real, d_model),
        x_sample.reshape(n_seq, d_model),
        jnp.tile(meta_tokens.astype(F32), (batch, 1)),
        jnp.zeros((ROW_ALIGN - n_used, d_model), F32)], axis=0)

    clf = cache_logf.reshape(depth, n_phys, SUBLANE, LANE)
    pad_rows = jnp.zeros((ROW_ALIGN - n_used, attn_w), F32)

    def row(v):
        return v.reshape(1, -1)

    outs = {k: [] for k in ("kp", "vp", "lfp", "pp", "ks", "vs", "lfs", "ps")}
    for l in range(depth):
        w_f = jnp.pad(w_in[l, :, 3 * attn_w + pool_w:], ((0, 0), (0, LANE - n_heads)))
        b_f = jnp.pad(b_forget[l].astype(F32), (0, LANE - n_heads)).reshape(1, LANE)

        h = _ffn_half(h, row(g_pre_ffn1[l]), row(g_post_ffn1[l]), ffn1_gate, ffn1_up, ffn1_down, l)

        z4, lf = _in_proj(h, row(g_pre_mix[l]), w_in, w_f, b_f, n_heads, l)

        lf_real_t = lf[:n_real].reshape(batch, seq, n_heads).transpose(0, 2, 1).reshape(batch * n_heads, seq)
        lf_meta_t = lf[m0:n_used].reshape(batch, N_META, n_heads).transpose(0, 2, 1).reshape(batch * n_heads, N_META)
        lf_meta_t = jnp.pad(lf_meta_t, ((0, 0), (0, LANE - N_META)))
        f_meta, f_real = _prompt_fcum(lf_meta_t, lf_real_t)
        a_real, a_meta = _prompt_attention(
            z4, f_real.reshape(batch * n_heads, 1, seq), f_meta.reshape(batch * n_heads, 1, LANE),
            batch, seq, n_heads, m0)

        q_s = z4[0, s0:m0].reshape(n_seq, n_heads, head_dim)
        k_s = z4[1, s0:m0].reshape(n_seq, n_heads, head_dim)
        v_s = z4[2, s0:m0].reshape(n_seq, n_heads, head_dim)
        p_s = z4[3, s0:m0]
        lf_s = lf[s0:m0]
        a_s = _decode_attention(page_table, q_s, k_s, v_s, lf_s.reshape(n_seq, n_heads, 1),
                                cache_k, cache_v, clf, l).reshape(n_seq, attn_w)

        wp = w_pool[l].astype(BF16)
        sc = row(pool_scale[l].astype(F32))
        pm_real = _pool_real(z4[3], wp, sc, batch, seq, m0)
        pm_meta, pm_s = _pool_small(z4[3, m0:n_used], p_s, state_pool[l].transpose(1, 0, 2), wp, sc)

        a_all = jnp.concatenate([a_real, a_s, a_meta, pad_rows], axis=0)
        pm_all = jnp.concatenate([pm_real, pm_s, pm_meta, pad_rows], axis=0)
        h = _out_proj(h, a_all, pm_all, w_out[l].astype(BF16), row(g_post_mix[l]))

        h = _ffn_half(h, row(g_pre_ffn2[l]), row(g_post_ffn2[l]), ffn2_gate, ffn2_up, ffn2_down, l)

        def seq_major(x_real, x_meta, tail):
            return jnp.concatenate(
                [x_meta.reshape((batch, N_META) + tail), x_real.reshape((batch, seq) + tail)], axis=1)

        outs["kp"].append(seq_major(z4[1, :n_real], z4[1, m0:n_used], (n_heads, head_dim)))
        outs["vp"].append(seq_major(z4[2, :n_real], z4[2, m0:n_used], (n_heads, head_dim)))
        outs["lfp"].append(seq_major(lf[:n_real], lf[m0:n_used], (n_heads,)))
        outs["pp"].append(z4[3, :n_real].reshape(batch, seq, pool_w)[:, seq - POOL_CTX:])
        outs["ks"].append(k_s.reshape(n_seq, 1, n_heads, head_dim))
        outs["vs"].append(v_s.reshape(n_seq, 1, n_heads, head_dim))
        outs["lfs"].append(lf_s.reshape(n_seq, 1, n_heads))
        outs["ps"].append(jnp.concatenate([state_pool[l][:, 1:], p_s[:, None, :]], axis=1))

    y_prompt = h[:n_real].reshape(batch, seq, d_model)
    y_sample = h[s0:m0].reshape(n_seq, 1, d_model)
    st = {k: jnp.stack(v, 0) for k, v in outs.items()}
    return (y_prompt, y_sample, st["kp"], st["vp"], st["lfp"], st["pp"],
            st["ks"], st["vs"], st["lfs"], st["ps"])
```

```python
import functools

import jax
import jax.numpy as jnp
from jax import lax
from jax.experimental import pallas as pl
from jax.experimental.pallas import tpu as pltpu

F32 = jnp.float32
BF16 = jnp.bfloat16

N_META = 16
HEAD_DIM = 128
PAGE_SIZE = 128
POOL_WINDOWS = (2, 4, 8, 16)
POOL_CTX = max(POOL_WINDOWS) - 1
RMS_EPS = 1e-6
NEG_INF = -1e30
ATTN_SCALE = HEAD_DIM ** -0.5

LANE = 128
SUBLANE = 8
ROW_TILE = 768
INPROJ_TILE = 1056
ROW_ALIGN = 8448
FF_TILE = 512
POOL_TILE = 512
Q_TILE = 256
VMEM_LIMIT = 56 * 1024 * 1024


def _rms(x, g):
    return x * lax.rsqrt(jnp.mean(x * x, axis=-1, keepdims=True) + RMS_EPS) * g


def _split3(x):
    hi = x.astype(BF16)
    r1 = x - hi.astype(F32)
    mid = r1.astype(BF16)
    lo = (r1 - mid.astype(F32)).astype(BF16)
    return hi, mid, lo


def _dot3(x, w01):
    hi, mid, lo = _split3(x)
    acc = jnp.dot(hi, w01, preferred_element_type=F32)
    acc = acc + jnp.dot(mid, w01, preferred_element_type=F32)
    return acc + jnp.dot(lo, w01, preferred_element_type=F32)


def _dot3_rhs(w01, x):
    hi, mid, lo = _split3(x)
    acc = jnp.dot(w01, hi, preferred_element_type=F32)
    acc = acc + jnp.dot(w01, mid, preferred_element_type=F32)
    return acc + jnp.dot(w01, lo, preferred_element_type=F32)


def _dot_nt(a, b):
    return lax.dot_general(a, b, (((1,), (1,)), ((), ())), preferred_element_type=F32)


def _dot_mixed(a, b):
    return lax.dot_general(a, b, (((1,), (0,)), ((), ())), preferred_element_type=F32)


def _ffn_kernel(x_ref, gpre_ref, gpost_ref, wg_ref, wu_ref, wd_ref, o_ref, xn_ref, *, last_valid):
    j = pl.program_id(1)
    last = pl.num_programs(1) - 1

    @pl.when(j == 0)
    def _():
        xn_ref[...] = _rms(x_ref[...], gpre_ref[...]).astype(BF16)
        o_ref[...] = jnp.zeros(o_ref.shape, F32)

    def swiglu_part(cols):
        xn = xn_ref[...]
        a = _dot_mixed(xn, wg_ref[:, :cols])
        b = _dot_mixed(xn, wu_ref[:, :cols])
        return _dot_mixed((a * jax.nn.sigmoid(a) * b).astype(BF16), wd_ref[:cols, :])

    @pl.when(j < last)
    def _():
        o_ref[...] += swiglu_part(wg_ref.shape[1])

    @pl.when(j == last)
    def _():
        y = o_ref[...] + swiglu_part(last_valid)
        o_ref[...] = x_ref[...] + 0.5 * _rms(y, gpost_ref[...])


def _ffn_half(h, g_pre, g_post, wg, wu, wd, layer):
    rows, d = h.shape
    ff = wg.shape[2]
    n_blocks = pl.cdiv(ff, FF_TILE)
    last_valid = ff - (n_blocks - 1) * FF_TILE
    assert last_valid % LANE == 0
    return pl.pallas_call(
        functools.partial(_ffn_kernel, last_valid=last_valid),
        grid=(rows // ROW_TILE, n_blocks),
        in_specs=[
            pl.BlockSpec((ROW_TILE, d), lambda i, j: (i, 0), pipeline_mode=pl.Buffered(1)),
            pl.BlockSpec((1, d), lambda i, j: (0, 0)),
            pl.BlockSpec((1, d), lambda i, j: (0, 0)),
            pl.BlockSpec((None, d, FF_TILE), lambda i, j: (layer, 0, j)),
            pl.BlockSpec((None, d, FF_TILE), lambda i, j: (layer, 0, j)),
            pl.BlockSpec((None, FF_TILE, d), lambda i, j: (layer, j, 0)),
        ],
        out_specs=pl.BlockSpec((ROW_TILE, d), lambda i, j: (i, 0)),
        out_shape=jax.ShapeDtypeStruct((rows, d), F32),
        scratch_shapes=[pltpu.VMEM((ROW_TILE, d), BF16)],
        compiler_params=pltpu.CompilerParams(
            dimension_semantics=("parallel", "arbitrary"), vmem_limit_bytes=VMEM_LIMIT),
        name="ffn_half",
    )(h, g_pre, g_post, wg, wu, wd)


def _inproj_kernel(h_ref, g_ref, w_ref, wf_ref, bf_ref, z_ref, lf_ref, u_ref):
    n = pl.program_id(1)

    @pl.when(n == 0)
    def _():
        u = _rms(h_ref[...], g_ref[...]).astype(BF16)
        u_ref[...] = u
        x = _dot_nt(u, wf_ref[...]) + bf_ref[...]
        lf = jnp.minimum(x, 0.0) - jnp.log1p(jnp.exp(-jnp.abs(x)))
        lf_ref[...] = lf[:, :lf_ref.shape[1]]

    z_ref[0] = _dot_nt(u_ref[...], w_ref[...])


def _in_proj(h, g, w_in_t, w_f_t, b_f, n_heads, layer):
    rows, d = h.shape
    col_tile = (w_in_t.shape[1] - n_heads) // 4
    return pl.pallas_call(
        _inproj_kernel,
        grid=(rows // INPROJ_TILE, 4),
        in_specs=[
            pl.BlockSpec((INPROJ_TILE, d), lambda i, n: (i, 0)),
            pl.BlockSpec((1, d), lambda i, n: (0, 0)),
            pl.BlockSpec((None, col_tile, d), lambda i, n: (layer, n, 0)),
            pl.BlockSpec((LANE, d), lambda i, n: (0, 0)),
            pl.BlockSpec((1, LANE), lambda i, n: (0, 0)),
        ],
        out_specs=[
            pl.BlockSpec((1, INPROJ_TILE, col_tile), lambda i, n: (n, i, 0)),
            pl.BlockSpec((INPROJ_TILE, n_heads), lambda i, n: (i, 0)),
        ],
        out_shape=[
            jax.ShapeDtypeStruct((4, rows, col_tile), F32),
            jax.ShapeDtypeStruct((rows, n_heads), F32),
        ],
        scratch_shapes=[pltpu.VMEM((INPROJ_TILE, d), BF16)],
        compiler_params=pltpu.CompilerParams(
            dimension_semantics=("parallel", "arbitrary"), vmem_limit_bytes=VMEM_LIMIT),
        name="in_proj",
    )(h, g, w_in_t, w_f_t, b_f)


def _ones_where(shape, pred):
    r = lax.broadcasted_iota(jnp.int32, shape, 0)
    c = lax.broadcasted_iota(jnp.int32, shape, 1)
    return jnp.where(pred(r, c), 1.0, 0.0).astype(BF16)


def _fcum_kernel(lfm_ref, lfr_ref, fm_ref, fr_ref):
    u = _ones_where((LANE, LANE), lambda r, c: r <= c)
    fm = _dot3(lfm_ref[...], u)
    fm_ref[...] = fm
    carry = fm[:, LANE - 1:LANE]
    for blk in range(lfr_ref.shape[1] // LANE):
        sl = slice(blk * LANE, (blk + 1) * LANE)
        f = _dot3(lfr_ref[:, sl], u) + carry
        fr_ref[:, sl] = f
        carry = f[:, LANE - 1:LANE]


def _prompt_fcum(lf_meta_t, lf_real_t):
    return pl.pallas_call(
        _fcum_kernel,
        out_shape=[jax.ShapeDtypeStruct(lf_meta_t.shape, F32), jax.ShapeDtypeStruct(lf_real_t.shape, F32)],
        name="prompt_fcum",
    )(lf_meta_t, lf_real_t)


def _softmax_pv(parts):
    m = None
    for s, _ in parts:
        mi = jnp.max(s, axis=-1, keepdims=True)
        m = mi if m is None else jnp.maximum(m, mi)
    l = None
    o = None
    for s, v in parts:
        p = jnp.exp(s - m)
        li = jnp.sum(p, axis=-1, keepdims=True)
        oi = jnp.dot(p.astype(BF16), v, preferred_element_type=F32)
        l = li if l is None else l + li
        o = oi if o is None else o + oi
    return o / l


def _causal(s, q0):
    r = lax.broadcasted_iota(jnp.int32, s.shape, 0) + q0
    c = lax.broadcasted_iota(jnp.int32, s.shape, 1)
    return jnp.where(c <= r, s, NEG_INF)


def _prompt_attn_kernel(q_ref, k_ref, v_ref, qm_ref, km_ref, vm_ref, fr_ref, fm_ref, o_ref, om_ref):
    seq = q_ref.shape[1]
    kb = k_ref[0].astype(BF16)
    vb = v_ref[0].astype(BF16)
    kmb = km_ref[0].astype(BF16)
    vmb = vm_ref[0].astype(BF16)
    f_real = fr_ref[0]
    f_meta = fm_ref[0][:, :N_META]

    qmb = (qm_ref[0] * ATTN_SCALE).astype(BF16)
    sm = _causal(_dot_nt(qmb, kmb) - f_meta, 0)
    om_ref[...] = _softmax_pv([(sm, vmb)])

    for qi in range(seq // Q_TILE):
        n = (qi + 1) * Q_TILE
        qb = (q_ref[0, qi * Q_TILE:n, :] * ATTN_SCALE).astype(BF16)
        s_real = _causal(_dot_nt(qb, kb[:n]) - f_real[:, :n], qi * Q_TILE)
        s_meta = _dot_nt(qb, kmb) - f_meta
        o_ref[qi * Q_TILE:n, :] = _softmax_pv([(s_real, vb[:n]), (s_meta, vmb)])


def _prompt_attention(z4, f_real, f_meta, batch, seq, n_heads, meta_row0):
    meta_blk0 = meta_row0 // N_META

    def zspec(which, meta):
        if meta:
            return pl.BlockSpec((1, N_META, HEAD_DIM), lambda b, h: (which, meta_blk0 + b, h))
        return pl.BlockSpec((1, seq, HEAD_DIM), lambda b, h: (which, b, h))

    return pl.pallas_call(
        _prompt_attn_kernel,
        grid=(batch, n_heads),
        in_specs=[
            zspec(0, False), zspec(1, False), zspec(2, False),
            zspec(0, True), zspec(1, True), zspec(2, True),
            pl.BlockSpec((1, 1, seq), lambda b, h: (b * n_heads + h, 0, 0)),
            pl.BlockSpec((1, 1, LANE), lambda b, h: (b * n_heads + h, 0, 0)),
        ],
        out_specs=[
            pl.BlockSpec((seq, HEAD_DIM), lambda b, h: (b, h)),
            pl.BlockSpec((N_META, HEAD_DIM), lambda b, h: (b, h)),
        ],
        out_shape=[
            jax.ShapeDtypeStruct((batch * seq, n_heads * HEAD_DIM), F32),
            jax.ShapeDtypeStruct((batch * N_META, n_heads * HEAD_DIM), F32),
        ],
        compiler_params=pltpu.CompilerParams(
            dimension_semantics=("parallel", "parallel"), vmem_limit_bytes=VMEM_LIMIT),
        name="prompt_attn",
    )(z4, z4, z4, z4, z4, z4, f_real, f_meta)


def _decode_kernel(pt_ref, q_ref, kn_ref, vn_ref, lfn_ref, ck_ref, cv_ref, clf_ref, o_ref,
                   kbuf, vbuf, lfbuf, f_ref, sems, *, layer, n_pages):
    b = pl.program_id(0)
    nb = pl.num_programs(0)
    half = n_pages // 2
    n_heads = q_ref.shape[1]
    keys_half = half * PAGE_SIZE * n_heads
    n_chunks = keys_half // LANE
    slot = b % 2

    def copies(seq, slot_):
        out = []
        for j in range(n_pages):
            page = pt_ref[seq * n_pages + j]
            dst_lanes = pl.ds((j // half) * HEAD_DIM, HEAD_DIM)
            out.append(pltpu.make_async_copy(
                ck_ref.at[layer, page], kbuf.at[slot_, j % half, :, :, dst_lanes], sems.at[0, slot_]))
            out.append(pltpu.make_async_copy(
                cv_ref.at[layer, page], vbuf.at[slot_, j % half, :, :, dst_lanes], sems.at[1, slot_]))
            out.append(pltpu.make_async_copy(
                clf_ref.at[layer, page], lfbuf.at[slot_, j], sems.at[2, slot_]))
        return out

    @pl.when(b == 0)
    def _():
        for c in copies(0, 0):
            c.start()

    @pl.when(b + 1 < nb)
    def _():
        for c in copies(b + 1, 1 - slot):
            c.start()

    for c in copies(b, slot):
        c.wait()

    x = lfbuf[slot].reshape(n_pages * SUBLANE, LANE)
    nr = x.shape[0]
    w_in_row = _ones_where((LANE, LANE), lambda r, c: (r % n_heads == c % n_heads) & (r <= c))
    w_row_tot = _ones_where((LANE, LANE), lambda r, c: r % n_heads == c % n_heads)
    l_strict = _ones_where((nr, nr), lambda r, c: c < r)
    f_all = _dot3(x, w_in_row) + _dot3_rhs(l_strict, _dot3(x, w_row_tot))
    f_ref[...] = f_all

    q = q_ref[0] * ATTN_SCALE
    zero = jnp.zeros_like(q)
    q2 = jnp.concatenate([jnp.concatenate([q, zero], axis=1),
                          jnp.concatenate([zero, q], axis=1)], axis=0)
    kmat = kbuf[slot].reshape(keys_half, 2 * HEAD_DIM)
    s2 = _dot_nt(q2, kmat)

    f_lo = jnp.concatenate(
        [jnp.broadcast_to(f_ref[c:c + 1, :], (n_heads, LANE)) for c in range(n_chunks)], axis=1)
    f_hi = jnp.concatenate(
        [jnp.broadcast_to(f_ref[n_chunks + c:n_chunks + c + 1, :], (n_heads, LANE)) for c in range(n_chunks)], axis=1)
    f2 = jnp.concatenate([f_lo, f_hi], axis=0)

    row = lax.broadcasted_iota(jnp.int32, s2.shape, 0)
    col = lax.broadcasted_iota(jnp.int32, s2.shape, 1)
    logit = jnp.where(col % n_heads == row % n_heads, s2 - f2, -jnp.inf)
    m2 = jnp.max(logit, axis=-1, keepdims=True)
    m = jnp.maximum(m2[:n_heads], m2[n_heads:])
    p = jnp.exp(logit - jnp.concatenate([m, m], axis=0))
    l2 = jnp.sum(p, axis=-1, keepdims=True)
    l = l2[:n_heads] + l2[n_heads:]
    vmat = vbuf[slot].reshape(keys_half, 2 * HEAD_DIM)
    o2 = jnp.dot(p, vmat, preferred_element_type=F32)
    o = o2[:n_heads, :HEAD_DIM] + o2[n_heads:, HEAD_DIM:]

    last = jnp.broadcast_to(f_ref[nr - 1:nr, :], (n_heads, LANE))
    r8 = lax.broadcasted_iota(jnp.int32, (n_heads, LANE), 0)
    c8 = lax.broadcasted_iota(jnp.int32, (n_heads, LANE), 1)
    f_past = jnp.sum(jnp.where(c8 == LANE - n_heads + r8, last, 0.0), axis=-1, keepdims=True)
    s_new = jnp.sum(q * kn_ref[0], axis=-1, keepdims=True)
    logit_new = s_new - (f_past + lfn_ref[0])
    m_fin = jnp.maximum(m, logit_new)
    alpha = jnp.exp(m - m_fin)
    p_new = jnp.exp(logit_new - m_fin)
    o_ref[0] = (alpha * o + p_new * vn_ref[0]) / (alpha * l + p_new)


def _decode_attention(page_table, q, k_new, v_new, lf_new, cache_k, cache_v, cache_lf, layer):
    n_seq, n_pages = page_table.shape
    _, n_heads, head_dim = q.shape
    half = n_pages // 2
    assert n_pages % 2 == 0 and PAGE_SIZE * n_heads == SUBLANE * LANE and head_dim == HEAD_DIM

    def row_map(b, pt):
        return (b, 0, 0)

    any_spec = pl.BlockSpec(memory_space=pl.ANY)
    grid_spec = pltpu.PrefetchScalarGridSpec(
        num_scalar_prefetch=1,
        grid=(n_seq,),
        in_specs=[
            pl.BlockSpec((1, n_heads, head_dim), row_map),
            pl.BlockSpec((1, n_heads, head_dim), row_map),
            pl.BlockSpec((1, n_heads, head_dim), row_map),
            pl.BlockSpec((1, n_heads, 1), row_map),
            any_spec, any_spec, any_spec,
        ],
        out_specs=pl.BlockSpec((1, n_heads, head_dim), row_map),
        scratch_shapes=[
            pltpu.VMEM((2, half, PAGE_SIZE, n_heads, 2 * head_dim), F32),
            pltpu.VMEM((2, half, PAGE_SIZE, n_heads, 2 * head_dim), F32),
            pltpu.VMEM((2, n_pages, SUBLANE, LANE), F32),
            pltpu.VMEM((n_pages * SUBLANE, LANE), F32),
            pltpu.SemaphoreType.DMA((3, 2)),
        ],
    )
    return pl.pallas_call(
        functools.partial(_decode_kernel, layer=layer, n_pages=n_pages),
        grid_spec=grid_spec,
        out_shape=jax.ShapeDtypeStruct((n_seq, n_heads, head_dim), F32),
        compiler_params=pltpu.CompilerParams(
            dimension_semantics=("arbitrary",), vmem_limit_bytes=VMEM_LIMIT),
        name="decode_attn",
    )(page_table.reshape(-1), q, k_new, v_new, lf_new, cache_k, cache_v, cache_lf)


def _pool_project(d_groups, wp_ref, sc_ref, o_ref):
    for g, d in enumerate(d_groups):
        gw = d.shape[1]
        cs = slice(g * gw, (g + 1) * gw)
        o_ref[:, cs] = jnp.dot(d.astype(BF16), wp_ref[g], preferred_element_type=F32) * sc_ref[:, cs]


def _pool_real_kernel(halo_ref, cur_ref, wp_ref, sc_ref, o_ref, ext_ref):
    rows = cur_ref.shape[0]
    gw = cur_ref.shape[1] // len(POOL_WINDOWS)
    ext_ref[0:N_META, :] = halo_ref[...]
    ext_ref[N_META:, :] = cur_ref[...]
    ds = []
    for g, w in enumerate(POOL_WINDOWS):
        cs = slice(g * gw, (g + 1) * gw)
        cur = ext_ref[N_META:N_META + rows, cs]
        s = cur
        for back in range(1, w):
            s = s + ext_ref[N_META - back:N_META - back + rows, cs]
        ds.append(s * (1.0 / w) - cur)
    _pool_project(ds, wp_ref, sc_ref, o_ref)


def _pool_real(z4, w_pool, scale, batch, seq, meta_row0):
    width = z4.shape[2]
    tiles = seq // POOL_TILE
    per16 = POOL_TILE // N_META
    meta_blk0 = meta_row0 // N_META

    def halo_map(b, i):
        return (3, jnp.where(i == 0, meta_blk0 + b, (b * tiles + i) * per16 - 1), 0)

    return pl.pallas_call(
        _pool_real_kernel,
        grid=(batch, tiles),
        in_specs=[
            pl.BlockSpec((None, N_META, width), halo_map),
            pl.BlockSpec((None, POOL_TILE, width), lambda b, i: (3, b * tiles + i, 0)),
            pl.BlockSpec(w_pool.shape, lambda b, i: (0, 0, 0)),
            pl.BlockSpec((1, width), lambda b, i: (0, 0)),
        ],
        out_specs=pl.BlockSpec((POOL_TILE, width), lambda b, i: (b * tiles + i, 0)),
        out_shape=jax.ShapeDtypeStruct((batch * seq, width), F32),
        scratch_shapes=[pltpu.VMEM((N_META + POOL_TILE, width), F32)],
        compiler_params=pltpu.CompilerParams(
            dimension_semantics=("parallel", "parallel"), vmem_limit_bytes=VMEM_LIMIT),
        name="pool_real",
    )(z4, z4, w_pool, scale)


def _pool_small_kernel(pm_ref, ps_ref, st_ref, wp_ref, sc_ref, om_ref, os_ref, ext_ref):
    width = pm_ref.shape[1]
    gw = width // len(POOL_WINDOWS)
    ext_ref[0:N_META, :] = jnp.zeros((N_META, width), F32)
    pos = lax.broadcasted_iota(jnp.int32, (N_META, 1), 0)
    for b in range(pm_ref.shape[0] // N_META):
        ext_ref[N_META:, :] = pm_ref[b * N_META:(b + 1) * N_META, :]
        ds = []
        for g, w in enumerate(POOL_WINDOWS):
            cs = slice(g * gw, (g + 1) * gw)
            cur = ext_ref[N_META:2 * N_META, cs]
            s = cur
            for back in range(1, w):
                s = s + ext_ref[N_META - back:2 * N_META - back, cs]
            cnt = jnp.minimum(pos + 1, w).astype(F32)
            ds.append(s / cnt - cur)
        _pool_project(ds, wp_ref, sc_ref, om_ref.at[b * N_META:(b + 1) * N_META, :])
    ds = []
    for g, w in enumerate(POOL_WINDOWS):
        cs = slice(g * gw, (g + 1) * gw)
        cur = ps_ref[:, cs]
        s = cur
        for back in range(1, w):
            s = s + st_ref[POOL_CTX - back, :, cs]
        ds.append(s * (1.0 / w) - cur)
    _pool_project(ds, wp_ref, sc_ref, os_ref)


def _pool_small(p_meta, p_sample, state_t, w_pool, scale):
    return pl.pallas_call(
        _pool_small_kernel,
        out_shape=[jax.ShapeDtypeStruct(p_meta.shape, F32), jax.ShapeDtypeStruct(p_sample.shape, F32)],
        scratch_shapes=[pltpu.VMEM((2 * N_META, p_meta.shape[1]), F32)],
        compiler_params=pltpu.CompilerParams(vmem_limit_bytes=VMEM_LIMIT),
        name="pool_small",
    )(p_meta, p_sample, state_t, w_pool, scale)


def _outproj_kernel(h_ref, a_ref, pm_ref, at_ref, pmt_ref, w_ref, g_ref, o_ref):
    i = pl.program_id(0)
    n_real_tiles = pl.num_programs(0) - 1

    def project(a_src, pm_src):
        aw = a_src.shape[1]
        mix = jnp.dot(a_src[...].astype(BF16), w_ref[0:aw, :], preferred_element_type=F32)
        mix = mix + jnp.dot(pm_src[...].astype(BF16), w_ref[aw:, :], preferred_element_type=F32)
        o_ref[...] = h_ref[...] + _rms(mix, g_ref[...])

    @pl.when(i < n_real_tiles)
    def _():
        project(a_ref, pm_ref)

    @pl.when(i == n_real_tiles)
    def _():
        project(at_ref, pmt_ref)


def _out_proj(h, a_real, pm_real, a_tail, pm_tail, w_out, g):
    rows, d = h.shape
    n_real = a_real.shape[0]
    tile = rows - n_real
    assert a_tail.shape[0] == tile and n_real % tile == 0
    n_real_tiles = n_real // tile

    def real_map(i):
        return (jnp.minimum(i, n_real_tiles - 1), 0)

    return pl.pallas_call(
        _outproj_kernel,
        grid=(n_real_tiles + 1,),
        in_specs=[
            pl.BlockSpec((tile, d), lambda i: (i, 0)),
            pl.BlockSpec((tile, a_real.shape[1]), real_map),
            pl.BlockSpec((tile, pm_real.shape[1]), real_map),
            pl.BlockSpec((tile, a_tail.shape[1]), lambda i: (0, 0)),
            pl.BlockSpec((tile, pm_tail.shape[1]), lambda i: (0, 0)),
            pl.BlockSpec(w_out.shape, lambda i: (0, 0)),
            pl.BlockSpec((1, d), lambda i: (0, 0)),
        ],
        out_specs=pl.BlockSpec((tile, d), lambda i: (i, 0)),
        out_shape=jax.ShapeDtypeStruct((rows, d), F32),
        compiler_params=pltpu.CompilerParams(
            dimension_semantics=("arbitrary",), vmem_limit_bytes=VMEM_LIMIT),
        name="out_proj",
    )(h, a_real, pm_real, a_tail, pm_tail, w_out, g)


def _kv_assemble_kernel(*refs, depth, batch, seq, n_heads, meta_row0):
    z_refs, (k_ref, v_ref, sem) = refs[:depth], refs[depth:]
    copies = []
    for l, z_ref in enumerate(z_refs):
        for slab, dst in ((1, k_ref), (2, v_ref)):
            for b in range(batch):
                for h in range(n_heads):
                    lanes = pl.ds(h * HEAD_DIM, HEAD_DIM)
                    copies.append(pltpu.make_async_copy(
                        z_ref.at[slab, pl.ds(meta_row0 + b * N_META, N_META), lanes],
                        dst.at[l, b, pl.ds(0, N_META), h, :], sem.at[0]))
                    copies.append(pltpu.make_async_copy(
                        z_ref.at[slab, pl.ds(b * seq, seq), lanes],
                        dst.at[l, b, pl.ds(N_META, seq), h, :], sem.at[1]))
    for c in copies:
        c.start()
    for c in copies:
        c.wait()


def _kv_assemble(z4s, batch, seq, n_heads, meta_row0):
    depth = len(z4s)
    shape = jax.ShapeDtypeStruct((depth, batch, N_META + seq, n_heads, HEAD_DIM), F32)
    any_spec = pl.BlockSpec(memory_space=pl.ANY)
    return pl.pallas_call(
        functools.partial(_kv_assemble_kernel, depth=depth, batch=batch, seq=seq, n_heads=n_heads,
                          meta_row0=meta_row0),
        in_specs=[any_spec] * depth,
        out_specs=[any_spec, any_spec],
        out_shape=[shape, shape],
        scratch_shapes=[pltpu.SemaphoreType.DMA((2,))],
        name="kv_assemble",
    )(*z4s)


def kernel(x_prompt, x_sample, cache_k, cache_v, cache_logf, state_pool, page_table, meta_tokens, w_in, b_forget, w_pool, pool_scale, w_out, ffn1_gate, ffn1_up, ffn1_down, ffn2_gate, ffn2_up, ffn2_down, g_pre_ffn1, g_post_ffn1, g_pre_mix, g_post_mix, g_pre_ffn2, g_post_ffn2):
    batch, seq, d_model = x_prompt.shape
    n_seq = x_sample.shape[0]
    depth, n_phys, page, n_heads, head_dim = cache_k.shape
    attn_w = n_heads * head_dim
    pool_w = state_pool.shape[-1]
    assert head_dim == HEAD_DIM and page == PAGE_SIZE and x_sample.shape[1] == 1
    assert attn_w == pool_w and w_in.shape[-1] == 3 * attn_w + pool_w + n_heads

    n_real = batch * seq
    s0 = n_real
    m0 = s0 + n_seq
    n_used = m0 + batch * N_META
    assert n_used <= ROW_ALIGN and m0 % N_META == 0 and seq % POOL_TILE == 0 and seq % Q_TILE == 0

    h = jnp.concatenate([
        x_prompt.reshape(n_real, d_model),
        x_sample.reshape(n_seq, d_model),
        jnp.tile(meta_tokens.astype(F32), (batch, 1)),
        jnp.zeros((ROW_ALIGN - n_used, d_model), F32)], axis=0)

    clf = cache_logf.reshape(depth, n_phys, SUBLANE, LANE)
    w_in_t = jnp.swapaxes(w_in, 1, 2)
    pad_rows = jnp.zeros((ROW_ALIGN - n_used, attn_w), F32)

    def row(v):
        return v.reshape(1, -1)

    outs = {k: [] for k in ("lfp", "pp", "ks", "vs", "lfs", "ps")}
    z4s = []
    for l in range(depth):
        w_f_t = jnp.pad(w_in_t[l, 3 * attn_w + pool_w:, :], ((0, LANE - n_heads), (0, 0)))
        b_f = jnp.pad(b_forget[l].astype(F32), (0, LANE - n_heads)).reshape(1, LANE)

        h = _ffn_half(h, row(g_pre_ffn1[l]), row(g_post_ffn1[l]), ffn1_gate, ffn1_up, ffn1_down, l)

        z4, lf = _in_proj(h, row(g_pre_mix[l]), w_in_t, w_f_t, b_f, n_heads, l)

        lf_real_t = lf[:n_real].reshape(batch, seq, n_heads).transpose(0, 2, 1).reshape(batch * n_heads, seq)
        lf_meta_t = lf[m0:n_used].reshape(batch, N_META, n_heads).transpose(0, 2, 1).reshape(batch * n_heads, N_META)
        lf_meta_t = jnp.pad(lf_meta_t, ((0, 0), (0, LANE - N_META)))
        f_meta, f_real = _prompt_fcum(lf_meta_t, lf_real_t)
        a_real, a_meta = _prompt_attention(
            z4, f_real.reshape(batch * n_heads, 1, seq), f_meta.reshape(batch * n_heads, 1, LANE),
            batch, seq, n_heads, m0)

        q_s = z4[0, s0:m0].reshape(n_seq, n_heads, head_dim)
        k_s = z4[1, s0:m0].reshape(n_seq, n_heads, head_dim)
        v_s = z4[2, s0:m0].reshape(n_seq, n_heads, head_dim)
        p_s = z4[3, s0:m0]
        lf_s = lf[s0:m0]
        a_s = _decode_attention(page_table, q_s, k_s, v_s, lf_s.reshape(n_seq, n_heads, 1),
                                cache_k, cache_v, clf, l).reshape(n_seq, attn_w)

        wp = w_pool[l].astype(BF16)
        sc = row(pool_scale[l].astype(F32))
        pm_real = _pool_real(z4, wp, sc, batch, seq, m0)
        pm_meta, pm_s = _pool_small(z4[3, m0:n_used], p_s, state_pool[l].transpose(1, 0, 2), wp, sc)

        a_tail = jnp.concatenate([a_s, a_meta, pad_rows], axis=0)
        pm_tail = jnp.concatenate([pm_s, pm_meta, pad_rows], axis=0)
        h = _out_proj(h, a_real, pm_real, a_tail, pm_tail, w_out[l].astype(BF16), row(g_post_mix[l]))

        h = _ffn_half(h, row(g_pre_ffn2[l]), row(g_post_ffn2[l]), ffn2_gate, ffn2_up, ffn2_down, l)

        def seq_major(x_real, x_meta, tail):
            return jnp.concatenate(
                [x_meta.reshape((batch, N_META) + tail), x_real.reshape((batch, seq) + tail)], axis=1)

        z4s.append(z4)
        outs["lfp"].append(seq_major(lf[:n_real], lf[m0:n_used], (n_heads,)))
        outs["pp"].append(jnp.stack([z4[3, (b + 1) * seq - POOL_CTX:(b + 1) * seq] for b in range(batch)], 0))
        outs["ks"].append(k_s.reshape(n_seq, 1, n_heads, head_dim))
        outs["vs"].append(v_s.reshape(n_seq, 1, n_heads, head_dim))
        outs["lfs"].append(lf_s.reshape(n_seq, 1, n_heads))
        outs["ps"].append(jnp.concatenate([state_pool[l][:, 1:], p_s[:, None, :]], axis=1))

    y_prompt = h[:n_real].reshape(batch, seq, d_model)
    y_sample = h[s0:m0].reshape(n_seq, 1, d_model)
    st = {k: jnp.stack(v, 0) for k, v in outs.items()}
    k_prompt, v_prompt = _kv_assemble(z4s, batch, seq, n_heads, m0)
    return (y_prompt, y_sample, k_prompt, v_prompt, st["lfp"], st["pp"],
            st["ks"], st["vs"], st["lfs"], st["ps"])
```

```python
import functools

import jax
import jax.numpy as jnp
from jax import lax
from jax.experimental import pallas as pl
from jax.experimental.pallas import tpu as pltpu

F32 = jnp.float32
BF16 = jnp.bfloat16

N_META = 16
HEAD_DIM = 128
PAGE_SIZE = 128
POOL_WINDOWS = (2, 4, 8, 16)
POOL_CTX = max(POOL_WINDOWS) - 1
RMS_EPS = 1e-6
NEG_INF = -1e30
ATTN_SCALE = HEAD_DIM ** -0.5

LANE = 128
SUBLANE = 8
ROW_TILE = 768
INPROJ_TILE = 1056
ROW_ALIGN = 8448
FF_TILE = 512
POOL_TILE = 512
Q_TILE = 256
VMEM_LIMIT = 56 * 1024 * 1024


def _rms(x, g):
    return x * lax.rsqrt(jnp.mean(x * x, axis=-1, keepdims=True) + RMS_EPS) * g


def _split3(x):
    hi = x.astype(BF16)
    r1 = x - hi.astype(F32)
    mid = r1.astype(BF16)
    lo = (r1 - mid.astype(F32)).astype(BF16)
    return hi, mid, lo


def _dot3(x, w01):
    hi, mid, lo = _split3(x)
    acc = jnp.dot(hi, w01, preferred_element_type=F32)
    acc = acc + jnp.dot(mid, w01, preferred_element_type=F32)
    return acc + jnp.dot(lo, w01, preferred_element_type=F32)


def _dot3_rhs(w01, x):
    hi, mid, lo = _split3(x)
    acc = jnp.dot(w01, hi, preferred_element_type=F32)
    acc = acc + jnp.dot(w01, mid, preferred_element_type=F32)
    return acc + jnp.dot(w01, lo, preferred_element_type=F32)


def _dot_nt(a, b):
    return lax.dot_general(a, b, (((1,), (1,)), ((), ())), preferred_element_type=F32)


def _dot_mixed(a, b):
    return lax.dot_general(a, b, (((1,), (0,)), ((), ())), preferred_element_type=F32)


def _ffn_kernel(x_ref, gpre_ref, gpost_ref, wg_ref, wu_ref, wd_ref, o_ref, xn_ref, *, last_valid):
    j = pl.program_id(1)
    last = pl.num_programs(1) - 1

    @pl.when(j == 0)
    def _():
        xn_ref[...] = _rms(x_ref[...], gpre_ref[...]).astype(BF16)
        o_ref[...] = jnp.zeros(o_ref.shape, F32)

    def swiglu_part(cols):
        xn = xn_ref[...]
        a = _dot_mixed(xn, wg_ref[:, :cols])
        b = _dot_mixed(xn, wu_ref[:, :cols])
        return _dot_mixed((a * jax.nn.sigmoid(a) * b).astype(BF16), wd_ref[:cols, :])

    @pl.when(j < last)
    def _():
        o_ref[...] += swiglu_part(wg_ref.shape[1])

    @pl.when(j == last)
    def _():
        y = o_ref[...] + swiglu_part(last_valid)
        o_ref[...] = x_ref[...] + 0.5 * _rms(y, gpost_ref[...])


def _ffn_half(h, g_pre, g_post, wg, wu, wd, layer):
    rows, d = h.shape
    ff = wg.shape[2]
    n_blocks = pl.cdiv(ff, FF_TILE)
    last_valid = ff - (n_blocks - 1) * FF_TILE
    assert last_valid % LANE == 0
    return pl.pallas_call(
        functools.partial(_ffn_kernel, last_valid=last_valid),
        grid=(rows // ROW_TILE, n_blocks),
        in_specs=[
            pl.BlockSpec((ROW_TILE, d), lambda i, j: (i, 0), pipeline_mode=pl.Buffered(1)),
            pl.BlockSpec((1, d), lambda i, j: (0, 0)),
            pl.BlockSpec((1, d), lambda i, j: (0, 0)),
            pl.BlockSpec((None, d, FF_TILE), lambda i, j: (layer, 0, j)),
            pl.BlockSpec((None, d, FF_TILE), lambda i, j: (layer, 0, j)),
            pl.BlockSpec((None, FF_TILE, d), lambda i, j: (layer, j, 0)),
        ],
        out_specs=pl.BlockSpec((ROW_TILE, d), lambda i, j: (i, 0)),
        out_shape=jax.ShapeDtypeStruct((rows, d), F32),
        scratch_shapes=[pltpu.VMEM((ROW_TILE, d), BF16)],
        compiler_params=pltpu.CompilerParams(
            dimension_semantics=("parallel", "arbitrary"), vmem_limit_bytes=VMEM_LIMIT),
        name="ffn_half",
    )(h, g_pre, g_post, wg, wu, wd)


def _inproj_kernel(h_ref, g_ref, w_ref, wf_ref, bf_ref, z_ref, lf_ref, u_ref):
    n = pl.program_id(1)

    @pl.when(n == 0)
    def _():
        u = _rms(h_ref[...], g_ref[...]).astype(BF16)
        u_ref[...] = u
        x = _dot_nt(u, wf_ref[...]) + bf_ref[...]
        lf = jnp.minimum(x, 0.0) - jnp.log1p(jnp.exp(-jnp.abs(x)))
        lf_ref[...] = lf[:, :lf_ref.shape[1]]

    z_ref[0] = _dot_nt(u_ref[...], w_ref[...])


def _in_proj(h, g, w_in_t, w_f_t, b_f, n_heads, layer):
    rows, d = h.shape
    col_tile = (w_in_t.shape[1] - n_heads) // 4
    return pl.pallas_call(
        _inproj_kernel,
        grid=(rows // INPROJ_TILE, 4),
        in_specs=[
            pl.BlockSpec((INPROJ_TILE, d), lambda i, n: (i, 0)),
            pl.BlockSpec((1, d), lambda i, n: (0, 0)),
            pl.BlockSpec((None, col_tile, d), lambda i, n: (layer, n, 0)),
            pl.BlockSpec((LANE, d), lambda i, n: (0, 0)),
            pl.BlockSpec((1, LANE), lambda i, n: (0, 0)),
        ],
        out_specs=[
            pl.BlockSpec((1, INPROJ_TILE, col_tile), lambda i, n: (n, i, 0)),
            pl.BlockSpec((INPROJ_TILE, n_heads), lambda i, n: (i, 0)),
        ],
        out_shape=[
            jax.ShapeDtypeStruct((4, rows, col_tile), F32),
            jax.ShapeDtypeStruct((rows, n_heads), F32),
        ],
        scratch_shapes=[pltpu.VMEM((INPROJ_TILE, d), BF16)],
        compiler_params=pltpu.CompilerParams(
            dimension_semantics=("parallel", "arbitrary"), vmem_limit_bytes=VMEM_LIMIT),
        name="in_proj",
    )(h, g, w_in_t, w_f_t, b_f)


def _ones_where(shape, pred):
    r = lax.broadcasted_iota(jnp.int32, shape, 0)
    c = lax.broadcasted_iota(jnp.int32, shape, 1)
    return jnp.where(pred(r, c), 1.0, 0.0).astype(BF16)


def _fcum_kernel(lfm_ref, lfr_ref, fm_ref, fr_ref):
    u = _ones_where((LANE, LANE), lambda r, c: r <= c)
    fm = _dot3(lfm_ref[...], u)
    fm_ref[...] = fm
    carry = fm[:, LANE - 1:LANE]
    for blk in range(lfr_ref.shape[1] // LANE):
        sl = slice(blk * LANE, (blk + 1) * LANE)
        f = _dot3(lfr_ref[:, sl], u) + carry
        fr_ref[:, sl] = f
        carry = f[:, LANE - 1:LANE]


def _prompt_fcum(lf_meta_t, lf_real_t):
    return pl.pallas_call(
        _fcum_kernel,
        out_shape=[jax.ShapeDtypeStruct(lf_meta_t.shape, F32), jax.ShapeDtypeStruct(lf_real_t.shape, F32)],
        name="prompt_fcum",
    )(lf_meta_t, lf_real_t)


def _softmax_pv(parts):
    m = None
    for s, _ in parts:
        mi = jnp.max(s, axis=-1, keepdims=True)
        m = mi if m is None else jnp.maximum(m, mi)
    l = None
    o = None
    for s, v in parts:
        p = jnp.exp(s - m)
        li = jnp.sum(p, axis=-1, keepdims=True)
        oi = jnp.dot(p.astype(BF16), v, preferred_element_type=F32)
        l = li if l is None else l + li
        o = oi if o is None else o + oi
    return o / l


def _causal(s, q0):
    r = lax.broadcasted_iota(jnp.int32, s.shape, 0) + q0
    c = lax.broadcasted_iota(jnp.int32, s.shape, 1)
    return jnp.where(c <= r, s, NEG_INF)


def _prompt_attn_kernel(q_ref, k_ref, v_ref, qm_ref, km_ref, vm_ref, fr_ref, fm_ref, o_ref, om_ref):
    seq = q_ref.shape[1]
    kb = k_ref[0].astype(BF16)
    vb = v_ref[0].astype(BF16)
    kmb = km_ref[0].astype(BF16)
    vmb = vm_ref[0].astype(BF16)
    f_real = fr_ref[0]
    f_meta = fm_ref[0][:, :N_META]

    qmb = (qm_ref[0] * ATTN_SCALE).astype(BF16)
    sm = _causal(_dot_nt(qmb, kmb) - f_meta, 0)
    om_ref[...] = _softmax_pv([(sm, vmb)])

    for qi in range(seq // Q_TILE):
        n = (qi + 1) * Q_TILE
        qb = (q_ref[0, qi * Q_TILE:n, :] * ATTN_SCALE).astype(BF16)
        s_real = _causal(_dot_nt(qb, kb[:n]) - f_real[:, :n], qi * Q_TILE)
        s_meta = _dot_nt(qb, kmb) - f_meta
        o_ref[qi * Q_TILE:n, :] = _softmax_pv([(s_real, vb[:n]), (s_meta, vmb)])


def _prompt_attention(z4, f_real, f_meta, batch, seq, n_heads, meta_row0):
    meta_blk0 = meta_row0 // N_META

    def zspec(which, meta):
        if meta:
            return pl.BlockSpec((1, N_META, HEAD_DIM), lambda b, h: (which, meta_blk0 + b, h))
        return pl.BlockSpec((1, seq, HEAD_DIM), lambda b, h: (which, b, h))

    return pl.pallas_call(
        _prompt_attn_kernel,
        grid=(batch, n_heads),
        in_specs=[
            zspec(0, False), zspec(1, False), zspec(2, False),
            zspec(0, True), zspec(1, True), zspec(2, True),
            pl.BlockSpec((1, 1, seq), lambda b, h: (b * n_heads + h, 0, 0)),
            pl.BlockSpec((1, 1, LANE), lambda b, h: (b * n_heads + h, 0, 0)),
        ],
        out_specs=[
            pl.BlockSpec((seq, HEAD_DIM), lambda b, h: (b, h)),
            pl.BlockSpec((N_META, HEAD_DIM), lambda b, h: (b, h)),
        ],
        out_shape=[
            jax.ShapeDtypeStruct((batch * seq, n_heads * HEAD_DIM), F32),
            jax.ShapeDtypeStruct((batch * N_META, n_heads * HEAD_DIM), F32),
        ],
        compiler_params=pltpu.CompilerParams(
            dimension_semantics=("parallel", "parallel"), vmem_limit_bytes=VMEM_LIMIT),
        name="prompt_attn",
    )(z4, z4, z4, z4, z4, z4, f_real, f_meta)


def _decode_kernel(pt_ref, q_ref, kn_ref, vn_ref, lfn_ref, ck_ref, cv_ref, clf_ref, o_ref,
                   kbuf, vbuf, lfbuf, f_ref, sems, *, layer, n_pages):
    b = pl.program_id(0)
    nb = pl.num_programs(0)
    half = n_pages // 2
    n_heads = q_ref.shape[1]
    keys_half = half * PAGE_SIZE * n_heads
    n_chunks = keys_half // LANE
    slot = b % 2

    def copies(seq, slot_):
        out = []
        for j in range(n_pages):
            page = pt_ref[seq * n_pages + j]
            dst_lanes = pl.ds((j // half) * HEAD_DIM, HEAD_DIM)
            out.append(pltpu.make_async_copy(
                ck_ref.at[layer, page], kbuf.at[slot_, j % half, :, :, dst_lanes], sems.at[0, slot_]))
            out.append(pltpu.make_async_copy(
                cv_ref.at[layer, page], vbuf.at[slot_, j % half, :, :, dst_lanes], sems.at[1, slot_]))
            out.append(pltpu.make_async_copy(
                clf_ref.at[layer, page], lfbuf.at[slot_, j], sems.at[2, slot_]))
        return out

    @pl.when(b == 0)
    def _():
        for c in copies(0, 0):
            c.start()

    @pl.when(b + 1 < nb)
    def _():
        for c in copies(b + 1, 1 - slot):
            c.start()

    for c in copies(b, slot):
        c.wait()

    x = lfbuf[slot].reshape(n_pages * SUBLANE, LANE)
    nr = x.shape[0]
    w_in_row = _ones_where((LANE, LANE), lambda r, c: (r % n_heads == c % n_heads) & (r <= c))
    w_row_tot = _ones_where((LANE, LANE), lambda r, c: r % n_heads == c % n_heads)
    l_strict = _ones_where((nr, nr), lambda r, c: c < r)
    f_all = _dot3(x, w_in_row) + _dot3_rhs(l_strict, _dot3(x, w_row_tot))
    f_ref[...] = f_all

    q = q_ref[0] * ATTN_SCALE
    zero = jnp.zeros_like(q)
    q2 = jnp.concatenate([jnp.concatenate([q, zero], axis=1),
                          jnp.concatenate([zero, q], axis=1)], axis=0)
    kmat = kbuf[slot].reshape(keys_half, 2 * HEAD_DIM)
    s2 = _dot_nt(q2, kmat)

    f_lo = jnp.concatenate(
        [jnp.broadcast_to(f_ref[c:c + 1, :], (n_heads, LANE)) for c in range(n_chunks)], axis=1)
    f_hi = jnp.concatenate(
        [jnp.broadcast_to(f_ref[n_chunks + c:n_chunks + c + 1, :], (n_heads, LANE)) for c in range(n_chunks)], axis=1)
    f2 = jnp.concatenate([f_lo, f_hi], axis=0)

    row = lax.broadcasted_iota(jnp.int32, s2.shape, 0)
    col = lax.broadcasted_iota(jnp.int32, s2.shape, 1)
    logit = jnp.where(col % n_heads == row % n_heads, s2 - f2, -jnp.inf)
    m2 = jnp.max(logit, axis=-1, keepdims=True)
    m = jnp.maximum(m2[:n_heads], m2[n_heads:])
    p = jnp.exp(logit - jnp.concatenate([m, m], axis=0))
    l2 = jnp.sum(p, axis=-1, keepdims=True)
    l = l2[:n_heads] + l2[n_heads:]
    vmat = vbuf[slot].reshape(keys_half, 2 * HEAD_DIM)
    o2 = jnp.dot(p, vmat, preferred_element_type=F32)
    o = o2[:n_heads, :HEAD_DIM] + o2[n_heads:, HEAD_DIM:]

    last = jnp.broadcast_to(f_ref[nr - 1:nr, :], (n_heads, LANE))
    r8 = lax.broadcasted_iota(jnp.int32, (n_heads, LANE), 0)
    c8 = lax.broadcasted_iota(jnp.int32, (n_heads, LANE), 1)
    f_past = jnp.sum(jnp.where(c8 == LANE - n_heads + r8, last, 0.0), axis=-1, keepdims=True)
    s_new = jnp.sum(q * kn_ref[0], axis=-1, keepdims=True)
    logit_new = s_new - (f_past + lfn_ref[0])
    m_fin = jnp.maximum(m, logit_new)
    alpha = jnp.exp(m - m_fin)
    p_new = jnp.exp(logit_new - m_fin)
    o_ref[0] = (alpha * o + p_new * vn_ref[0]) / (alpha * l + p_new)


def _decode_attention(page_table, q, k_new, v_new, lf_new, cache_k, cache_v, cache_lf, layer):
    n_seq, n_pages = page_table.shape
    _, n_heads, head_dim = q.shape
    half = n_pages // 2
    assert n_pages % 2 == 0 and PAGE_SIZE * n_heads == SUBLANE * LANE and head_dim == HEAD_DIM

    def row_map(b, pt):
        return (b, 0, 0)

    any_spec = pl.BlockSpec(memory_space=pl.ANY)
    grid_spec = pltpu.PrefetchScalarGridSpec(
        num_scalar_prefetch=1,
        grid=(n_seq,),
        in_specs=[
            pl.BlockSpec((1, n_heads, head_dim), row_map),
            pl.BlockSpec((1, n_heads, head_dim), row_map),
            pl.BlockSpec((1, n_heads, head_dim), row_map),
            pl.BlockSpec((1, n_heads, 1), row_map),
            any_spec, any_spec, any_spec,
        ],
        out_specs=pl.BlockSpec((1, n_heads, head_dim), row_map),
        scratch_shapes=[
            pltpu.VMEM((2, half, PAGE_SIZE, n_heads, 2 * head_dim), F32),
            pltpu.VMEM((2, half, PAGE_SIZE, n_heads, 2 * head_dim), F32),
            pltpu.VMEM((2, n_pages, SUBLANE, LANE), F32),
            pltpu.VMEM((n_pages * SUBLANE, LANE), F32),
            pltpu.SemaphoreType.DMA((3, 2)),
        ],
    )
    return pl.pallas_call(
        functools.partial(_decode_kernel, layer=layer, n_pages=n_pages),
        grid_spec=grid_spec,
        out_shape=jax.ShapeDtypeStruct((n_seq, n_heads, head_dim), F32),
        compiler_params=pltpu.CompilerParams(
            dimension_semantics=("arbitrary",), vmem_limit_bytes=VMEM_LIMIT),
        name="decode_attn",
    )(page_table.reshape(-1), q, k_new, v_new, lf_new, cache_k, cache_v, cache_lf)


def _pool_project(d_groups, wp_ref, sc_ref, o_ref):
    for g, d in enumerate(d_groups):
        gw = d.shape[1]
        cs = slice(g * gw, (g + 1) * gw)
        o_ref[:, cs] = jnp.dot(d.astype(BF16), wp_ref[g], preferred_element_type=F32) * sc_ref[:, cs]


def _pool_real_kernel(halo_ref, cur_ref, wp_ref, sc_ref, o_ref, ext_ref):
    rows = cur_ref.shape[0]
    gw = cur_ref.shape[1] // len(POOL_WINDOWS)
    ext_ref[0:N_META, :] = halo_ref[...]
    ext_ref[N_META:, :] = cur_ref[...]
    ds = []
    for g, w in enumerate(POOL_WINDOWS):
        cs = slice(g * gw, (g + 1) * gw)
        cur = ext_ref[N_META:N_META + rows, cs]
        s = cur
        for back in range(1, w):
            s = s + ext_ref[N_META - back:N_META - back + rows, cs]
        ds.append(s * (1.0 / w) - cur)
    _pool_project(ds, wp_ref, sc_ref, o_ref)


def _pool_real(z4, w_pool, scale, batch, seq, meta_row0):
    width = z4.shape[2]
    tiles = seq // POOL_TILE
    per16 = POOL_TILE // N_META
    meta_blk0 = meta_row0 // N_META

    def halo_map(b, i):
        return (3, jnp.where(i == 0, meta_blk0 + b, (b * tiles + i) * per16 - 1), 0)

    return pl.pallas_call(
        _pool_real_kernel,
        grid=(batch, tiles),
        in_specs=[
            pl.BlockSpec((None, N_META, width), halo_map),
            pl.BlockSpec((None, POOL_TILE, width), lambda b, i: (3, b * tiles + i, 0)),
            pl.BlockSpec(w_pool.shape, lambda b, i: (0, 0, 0)),
            pl.BlockSpec((1, width), lambda b, i: (0, 0)),
        ],
        out_specs=pl.BlockSpec((POOL_TILE, width), lambda b, i: (b * tiles + i, 0)),
        out_shape=jax.ShapeDtypeStruct((batch * seq, width), F32),
        scratch_shapes=[pltpu.VMEM((N_META + POOL_TILE, width), F32)],
        compiler_params=pltpu.CompilerParams(
            dimension_semantics=("parallel", "parallel"), vmem_limit_bytes=VMEM_LIMIT),
        name="pool_real",
    )(z4, z4, w_pool, scale)


def _pool_small_kernel(pm_ref, ps_ref, st_ref, wp_ref, sc_ref, om_ref, os_ref, ext_ref):
    width = pm_ref.shape[1]
    gw = width // len(POOL_WINDOWS)
    ext_ref[0:N_META, :] = jnp.zeros((N_META, width), F32)
    pos = lax.broadcasted_iota(jnp.int32, (N_META, 1), 0)
    for b in range(pm_ref.shape[0] // N_META):
        ext_ref[N_META:, :] = pm_ref[b * N_META:(b + 1) * N_META, :]
        ds = []
        for g, w in enumerate(POOL_WINDOWS):
            cs = slice(g * gw, (g + 1) * gw)
            cur = ext_ref[N_META:2 * N_META, cs]
            s = cur
            for back in range(1, w):
                s = s + ext_ref[N_META - back:2 * N_META - back, cs]
            cnt = jnp.minimum(pos + 1, w).astype(F32)
            ds.append(s / cnt - cur)
        _pool_project(ds, wp_ref, sc_ref, om_ref.at[b * N_META:(b + 1) * N_META, :])
    ds = []
    for g, w in enumerate(POOL_WINDOWS):
        cs = slice(g * gw, (g + 1) * gw)
        cur = ps_ref[:, cs]
        s = cur
        for back in range(1, w):
            s = s + st_ref[POOL_CTX - back, :, cs]
        ds.append(s * (1.0 / w) - cur)
    _pool_project(ds, wp_ref, sc_ref, os_ref)


def _pool_small(p_meta, p_sample, state_t, w_pool, scale):
    return pl.pallas_call(
        _pool_small_kernel,
        out_shape=[jax.ShapeDtypeStruct(p_meta.shape, F32), jax.ShapeDtypeStruct(p_sample.shape, F32)],
        scratch_shapes=[pltpu.VMEM((2 * N_META, p_meta.shape[1]), F32)],
        compiler_params=pltpu.CompilerParams(vmem_limit_bytes=VMEM_LIMIT),
        name="pool_small",
    )(p_meta, p_sample, state_t, w_pool, scale)


def _outproj_kernel(h_ref, a_ref, pm_ref, at_ref, pmt_ref, w_ref, g_ref, o_ref):
    i = pl.program_id(0)
    n_real_tiles = pl.num_programs(0) - 1

    def project(a_src, pm_src):
        aw = a_src.shape[1]
        mix = jnp.dot(a_src[...].astype(BF16), w_ref[0:aw, :], preferred_element_type=F32)
        mix = mix + jnp.dot(pm_src[...].astype(BF16), w_ref[aw:, :], preferred_element_type=F32)
        o_ref[...] = h_ref[...] + _rms(mix, g_ref[...])

    @pl.when(i < n_real_tiles)
    def _():
        project(a_ref, pm_ref)

    @pl.when(i == n_real_tiles)
    def _():
        project(at_ref, pmt_ref)


def _out_proj(h, a_real, pm_real, a_tail, pm_tail, w_out, g):
    rows, d = h.shape
    n_real = a_real.shape[0]
    tile = rows - n_real
    assert a_tail.shape[0] == tile and n_real % tile == 0
    n_real_tiles = n_real // tile

    def real_map(i):
        return (jnp.minimum(i, n_real_tiles - 1), 0)

    return pl.pallas_call(
        _outproj_kernel,
        grid=(n_real_tiles + 1,),
        in_specs=[
            pl.BlockSpec((tile, d), lambda i: (i, 0)),
            pl.BlockSpec((tile, a_real.shape[1]), real_map),
            pl.BlockSpec((tile, pm_real.shape[1]), real_map),
            pl.BlockSpec((tile, a_tail.shape[1]), lambda i: (0, 0)),
            pl.BlockSpec((tile, pm_tail.shape[1]), lambda i: (0, 0)),
            pl.BlockSpec(w_out.shape, lambda i: (0, 0)),
            pl.BlockSpec((1, d), lambda i: (0, 0)),
        ],
        out_specs=pl.BlockSpec((tile, d), lambda i: (i, 0)),
        out_shape=jax.ShapeDtypeStruct((rows, d), F32),
        compiler_params=pltpu.CompilerParams(
            dimension_semantics=("arbitrary",), vmem_limit_bytes=VMEM_LIMIT),
        name="out_proj",
    )(h, a_real, pm_real, a_tail, pm_tail, w_out, g)


def _kv_assemble_kernel(*refs, depth, n_heads):
    z_refs = refs[:2 * depth]
    k_ref, v_ref = refs[2 * depth:]
    layer = pl.program_id(0)
    tensor = pl.program_id(2)

    def retile(real_ref, meta_ref, o_ref):
        for h in range(n_heads):
            lanes = slice(h * HEAD_DIM, (h + 1) * HEAD_DIM)
            o_ref[0:N_META, h, :] = meta_ref[:, lanes]
            o_ref[N_META:, h, :] = real_ref[:, lanes]

    for l in range(depth):
        for t, o_ref in enumerate((k_ref, v_ref)):
            @pl.when((layer == l) & (tensor == t))
            def _():
                retile(z_refs[2 * l], z_refs[2 * l + 1], o_ref)


def _kv_assemble(z4s, batch, seq, n_heads, meta_row0):
    depth = len(z4s)
    width = z4s[0].shape[2]
    meta_blk0 = meta_row0 // N_META
    shape = jax.ShapeDtypeStruct((depth, batch, N_META + seq, n_heads, HEAD_DIM), F32)

    in_specs = []
    args = []
    for l, z in enumerate(z4s):
        def pick(l_, b, t, l=l):
            bb = jnp.where(l_ == l, b, jnp.where(l_ < l, 0, batch - 1))
            tt = jnp.where(l_ == l, t, jnp.where(l_ < l, 0, 1))
            return bb, tt

        def real_map(l_, b, t, pick=pick):
            bb, tt = pick(l_, b, t)
            return (1 + tt, bb, 0)

        def meta_map(l_, b, t, pick=pick):
            bb, tt = pick(l_, b, t)
            return (1 + tt, meta_blk0 + bb, 0)

        in_specs.append(pl.BlockSpec((None, seq, width), real_map, pipeline_mode=pl.Buffered(1)))
        in_specs.append(pl.BlockSpec((None, N_META, width), meta_map))
        args += [z, z]

    out_spec = pl.BlockSpec((None, None, N_META + seq, n_heads, HEAD_DIM), lambda l_, b, t: (l_, b, 0, 0, 0))
    return pl.pallas_call(
        functools.partial(_kv_assemble_kernel, depth=depth, n_heads=n_heads),
        grid=(depth, batch, 2),
        in_specs=in_specs,
        out_specs=[out_spec, out_spec],
        out_shape=[shape, shape],
        compiler_params=pltpu.CompilerParams(
            dimension_semantics=("arbitrary", "arbitrary", "arbitrary"), vmem_limit_bytes=VMEM_LIMIT),
        name="kv_assemble",
    )(*args)


def kernel(x_prompt, x_sample, cache_k, cache_v, cache_logf, state_pool, page_table, meta_tokens, w_in, b_forget, w_pool, pool_scale, w_out, ffn1_gate, ffn1_up, ffn1_down, ffn2_gate, ffn2_up, ffn2_down, g_pre_ffn1, g_post_ffn1, g_pre_mix, g_post_mix, g_pre_ffn2, g_post_ffn2):
    batch, seq, d_model = x_prompt.shape
    n_seq = x_sample.shape[0]
    depth, n_phys, page, n_heads, head_dim = cache_k.shape
    attn_w = n_heads * head_dim
    pool_w = state_pool.shape[-1]
    assert head_dim == HEAD_DIM and page == PAGE_SIZE and x_sample.shape[1] == 1
    assert attn_w == pool_w and w_in.shape[-1] == 3 * attn_w + pool_w + n_heads

    n_real = batch * seq
    s0 = n_real
    m0 = s0 + n_seq
    n_used = m0 + batch * N_META
    assert n_used <= ROW_ALIGN and m0 % N_META == 0 and seq % POOL_TILE == 0 and seq % Q_TILE == 0

    h = jnp.concatenate([
        x_prompt.reshape(n_real, d_model),
        x_sample.reshape(n_seq, d_model),
        jnp.tile(meta_tokens.astype(F32), (batch, 1)),
        jnp.zeros((ROW_ALIGN - n_used, d_model), F32)], axis=0)

    clf = cache_logf.reshape(depth, n_phys, SUBLANE, LANE)
    w_in_t = jnp.swapaxes(w_in, 1, 2)
    pad_rows = jnp.zeros((ROW_ALIGN - n_used, attn_w), F32)

    def row(v):
        return v.reshape(1, -1)

    outs = {k: [] for k in ("lfp", "pp", "ks", "vs", "lfs", "ps")}
    z4s = []
    for l in range(depth):
        w_f_t = jnp.pad(w_in_t[l, 3 * attn_w + pool_w:, :], ((0, LANE - n_heads), (0, 0)))
        b_f = jnp.pad(b_forget[l].astype(F32), (0, LANE - n_heads)).reshape(1, LANE)

        h = _ffn_half(h, row(g_pre_ffn1[l]), row(g_post_ffn1[l]), ffn1_gate, ffn1_up, ffn1_down, l)

        z4, lf = _in_proj(h, row(g_pre_mix[l]), w_in_t, w_f_t, b_f, n_heads, l)

        lf_real_t = lf[:n_real].reshape(batch, seq, n_heads).transpose(0, 2, 1).reshape(batch * n_heads, seq)
        lf_meta_t = lf[m0:n_used].reshape(batch, N_META, n_heads).transpose(0, 2, 1).reshape(batch * n_heads, N_META)
        lf_meta_t = jnp.pad(lf_meta_t, ((0, 0), (0, LANE - N_META)))
        f_meta, f_real = _prompt_fcum(lf_meta_t, lf_real_t)
        a_real, a_meta = _prompt_attention(
            z4, f_real.reshape(batch * n_heads, 1, seq), f_meta.reshape(batch * n_heads, 1, LANE),
            batch, seq, n_heads, m0)

        q_s = z4[0, s0:m0].reshape(n_seq, n_heads, head_dim)
        k_s = z4[1, s0:m0].reshape(n_seq, n_heads, head_dim)
        v_s = z4[2, s0:m0].reshape(n_seq, n_heads, head_dim)
        p_s = z4[3, s0:m0]
        lf_s = lf[s0:m0]
        a_s = _decode_attention(page_table, q_s, k_s, v_s, lf_s.reshape(n_seq, n_heads, 1),
                                cache_k, cache_v, clf, l).reshape(n_seq, attn_w)

        wp = w_pool[l].astype(BF16)
        sc = row(pool_scale[l].astype(F32))
        pm_real = _pool_real(z4, wp, sc, batch, seq, m0)
        pm_meta, pm_s = _pool_small(z4[3, m0:n_used], p_s, state_pool[l].transpose(1, 0, 2), wp, sc)

        a_tail = jnp.concatenate([a_s, a_meta, pad_rows], axis=0)
        pm_tail = jnp.concatenate([pm_s, pm_meta, pad_rows], axis=0)
        h = _out_proj(h, a_real, pm_real, a_tail, pm_tail, w_out[l].astype(BF16), row(g_post_mix[l]))

        h = _ffn_half(h, row(g_pre_ffn2[l]), row(g_post_ffn2[l]), ffn2_gate, ffn2_up, ffn2_down, l)

        def seq_major(x_real, x_meta, tail):
            return jnp.concatenate(
                [x_meta.reshape((batch, N_META) + tail), x_real.reshape((batch, seq) + tail)], axis=1)

        z4s.append(z4)
        outs["lfp"].append(seq_major(lf[:n_real], lf[m0:n_used], (n_heads,)))
        outs["pp"].append(jnp.stack([z4[3, (b + 1) * seq - POOL_CTX:(b + 1) * seq] for b in range(batch)], 0))
        outs["ks"].append(k_s.reshape(n_seq, 1, n_heads, head_dim))
        outs["vs"].append(v_s.reshape(n_seq, 1, n_heads, head_dim))
        outs["lfs"].append(lf_s.reshape(n_seq, 1, n_heads))
        outs["ps"].append(jnp.concatenate([state_pool[l][:, 1:], p_s[:, None, :]], axis=1))

    y_prompt = h[:n_real].reshape(batch, seq, d_model)
    y_sample = h[s0:m0].reshape(n_seq, 1, d_model)
    st = {k: jnp.stack(v, 0) for k, v in outs.items()}
    k_prompt, v_prompt = _kv_assemble(z4s, batch, seq, n_heads, m0)
    return (y_prompt, y_sample, k_prompt, v_prompt, st["lfp"], st["pp"],
            st["ks"], st["vs"], st["lfs"], st["ps"])
```

```python
import functools

import jax
import jax.numpy as jnp
from jax import lax
from jax.experimental import pallas as pl
from jax.experimental.pallas import tpu as pltpu

F32 = jnp.float32
BF16 = jnp.bfloat16

N_META = 16
HEAD_DIM = 128
PAGE_SIZE = 128
POOL_WINDOWS = (2, 4, 8, 16)
POOL_CTX = max(POOL_WINDOWS) - 1
RMS_EPS = 1e-6
NEG_INF = -1e30
ATTN_SCALE = HEAD_DIM ** -0.5

LANE = 128
SUBLANE = 8
ROW_TILE = 768
INPROJ_TILE = 1056
ROW_ALIGN = 8448
FF_TILE = 512
POOL_TILE = 512
Q_TILE = 256
VMEM_LIMIT = 56 * 1024 * 1024


def _rms(x, g):
    return x * lax.rsqrt(jnp.mean(x * x, axis=-1, keepdims=True) + RMS_EPS) * g


def _split3(x):
    hi = x.astype(BF16)
    r1 = x - hi.astype(F32)
    mid = r1.astype(BF16)
    lo = (r1 - mid.astype(F32)).astype(BF16)
    return hi, mid, lo


def _dot3(x, w01):
    hi, mid, lo = _split3(x)
    acc = jnp.dot(hi, w01, preferred_element_type=F32)
    acc = acc + jnp.dot(mid, w01, preferred_element_type=F32)
    return acc + jnp.dot(lo, w01, preferred_element_type=F32)


def _dot3_rhs(w01, x):
    hi, mid, lo = _split3(x)
    acc = jnp.dot(w01, hi, preferred_element_type=F32)
    acc = acc + jnp.dot(w01, mid, preferred_element_type=F32)
    return acc + jnp.dot(w01, lo, preferred_element_type=F32)


def _dot_nt(a, b):
    return lax.dot_general(a, b, (((1,), (1,)), ((), ())), preferred_element_type=F32)


def _dot_mixed(a, b):
    return lax.dot_general(a, b, (((1,), (0,)), ((), ())), preferred_element_type=F32)


def _ffn_kernel(x_ref, gpre_ref, gpost_ref, wg_ref, wu_ref, wd_ref, o_ref, xn_ref, *, last_valid):
    j = pl.program_id(1)
    last = pl.num_programs(1) - 1

    @pl.when(j == 0)
    def _():
        xn_ref[...] = _rms(x_ref[...], gpre_ref[...]).astype(BF16)
        o_ref[...] = jnp.zeros(o_ref.shape, F32)

    def swiglu_part(cols):
        xn = xn_ref[...]
        a = _dot_mixed(xn, wg_ref[:, :cols])
        b = _dot_mixed(xn, wu_ref[:, :cols])
        return _dot_mixed((a * jax.nn.sigmoid(a) * b).astype(BF16), wd_ref[:cols, :])

    @pl.when(j < last)
    def _():
        o_ref[...] += swiglu_part(wg_ref.shape[1])

    @pl.when(j == last)
    def _():
        y = o_ref[...] + swiglu_part(last_valid)
        o_ref[...] = x_ref[...] + 0.5 * _rms(y, gpost_ref[...])


def _ffn_half(h, g_pre, g_post, wg, wu, wd, layer):
    rows, d = h.shape
    ff = wg.shape[2]
    n_blocks = pl.cdiv(ff, FF_TILE)
    last_valid = ff - (n_blocks - 1) * FF_TILE
    assert last_valid % LANE == 0
    return pl.pallas_call(
        functools.partial(_ffn_kernel, last_valid=last_valid),
        grid=(rows // ROW_TILE, n_blocks),
        in_specs=[
            pl.BlockSpec((ROW_TILE, d), lambda i, j: (i, 0), pipeline_mode=pl.Buffered(1)),
            pl.BlockSpec((1, d), lambda i, j: (0, 0)),
            pl.BlockSpec((1, d), lambda i, j: (0, 0)),
            pl.BlockSpec((None, d, FF_TILE), lambda i, j: (layer, 0, j)),
            pl.BlockSpec((None, d, FF_TILE), lambda i, j: (layer, 0, j)),
            pl.BlockSpec((None, FF_TILE, d), lambda i, j: (layer, j, 0)),
        ],
        out_specs=pl.BlockSpec((ROW_TILE, d), lambda i, j: (i, 0)),
        out_shape=jax.ShapeDtypeStruct((rows, d), F32),
        scratch_shapes=[pltpu.VMEM((ROW_TILE, d), BF16)],
        compiler_params=pltpu.CompilerParams(
            dimension_semantics=("parallel", "arbitrary"), vmem_limit_bytes=VMEM_LIMIT),
        name="ffn_half",
    )(h, g_pre, g_post, wg, wu, wd)


def _inproj_kernel(h_ref, g_ref, w_ref, wf_ref, bf_ref, z_ref, lf_ref, u_ref):
    n = pl.program_id(1)

    @pl.when(n == 0)
    def _():
        u = _rms(h_ref[...], g_ref[...]).astype(BF16)
        u_ref[...] = u
        x = _dot_nt(u, wf_ref[...]) + bf_ref[...]
        lf = jnp.minimum(x, 0.0) - jnp.log1p(jnp.exp(-jnp.abs(x)))
        lf_ref[...] = lf[:, :lf_ref.shape[1]]

    z_ref[0] = _dot_nt(u_ref[...], w_ref[...])


def _in_proj(h, g, w_in_t, w_f_t, b_f, n_heads, layer):
    rows, d = h.shape
    col_tile = (w_in_t.shape[1] - n_heads) // 4
    return pl.pallas_call(
        _inproj_kernel,
        grid=(rows // INPROJ_TILE, 4),
        in_specs=[
            pl.BlockSpec((INPROJ_TILE, d), lambda i, n: (i, 0)),
            pl.BlockSpec((1, d), lambda i, n: (0, 0)),
            pl.BlockSpec((None, col_tile, d), lambda i, n: (layer, n, 0)),
            pl.BlockSpec((LANE, d), lambda i, n: (0, 0)),
            pl.BlockSpec((1, LANE), lambda i, n: (0, 0)),
        ],
        out_specs=[
            pl.BlockSpec((1, INPROJ_TILE, col_tile), lambda i, n: (n, i, 0)),
            pl.BlockSpec((INPROJ_TILE, n_heads), lambda i, n: (i, 0)),
        ],
        out_shape=[
            jax.ShapeDtypeStruct((4, rows, col_tile), F32),
            jax.ShapeDtypeStruct((rows, n_heads), F32),
        ],
        scratch_shapes=[pltpu.VMEM((INPROJ_TILE, d), BF16)],
        compiler_params=pltpu.CompilerParams(
            dimension_semantics=("parallel", "arbitrary"), vmem_limit_bytes=VMEM_LIMIT),
        name="in_proj",
    )(h, g, w_in_t, w_f_t, b_f)


def _ones_where(shape, pred):
    r = lax.broadcasted_iota(jnp.int32, shape, 0)
    c = lax.broadcasted_iota(jnp.int32, shape, 1)
    return jnp.where(pred(r, c), 1.0, 0.0).astype(BF16)


def _fcum_kernel(lfm_ref, lfr_ref, fm_ref, fr_ref):
    u = _ones_where((LANE, LANE), lambda r, c: r <= c)
    fm = _dot3(lfm_ref[...], u)
    fm_ref[...] = fm
    carry = fm[:, LANE - 1:LANE]
    for blk in range(lfr_ref.shape[1] // LANE):
        sl = slice(blk * LANE, (blk + 1) * LANE)
        f = _dot3(lfr_ref[:, sl], u) + carry
        fr_ref[:, sl] = f
        carry = f[:, LANE - 1:LANE]


def _prompt_fcum(lf_meta_t, lf_real_t):
    return pl.pallas_call(
        _fcum_kernel,
        out_shape=[jax.ShapeDtypeStruct(lf_meta_t.shape, F32), jax.ShapeDtypeStruct(lf_real_t.shape, F32)],
        name="prompt_fcum",
    )(lf_meta_t, lf_real_t)


def _softmax_pv(parts):
    m = None
    for s, _ in parts:
        mi = jnp.max(s, axis=-1, keepdims=True)
        m = mi if m is None else jnp.maximum(m, mi)
    l = None
    o = None
    for s, v in parts:
        p = jnp.exp(s - m)
        li = jnp.sum(p, axis=-1, keepdims=True)
        oi = jnp.dot(p.astype(BF16), v, preferred_element_type=F32)
        l = li if l is None else l + li
        o = oi if o is None else o + oi
    return o / l


def _causal(s, q0):
    r = lax.broadcasted_iota(jnp.int32, s.shape, 0) + q0
    c = lax.broadcasted_iota(jnp.int32, s.shape, 1)
    return jnp.where(c <= r, s, NEG_INF)


def _prompt_attn_tiles(q_ref, k_ref, v_ref, qm_ref, km_ref, vm_ref, fr_ref, fm_ref, o_ref, om_ref,
                        kb_ref, vb_ref, tiles, first):
    kmb = km_ref[0].astype(BF16)
    vmb = vm_ref[0].astype(BF16)
    f_meta = fm_ref[0][:, :N_META]

    if first:
        kb_ref[...] = k_ref[0].astype(BF16)
        vb_ref[...] = v_ref[0].astype(BF16)
        qmb = (qm_ref[0] * ATTN_SCALE).astype(BF16)
        sm = _causal(_dot_nt(qmb, kmb) - f_meta, 0)
        om_ref[...] = _softmax_pv([(sm, vmb)])

    for qi in tiles:
        q0 = qi * Q_TILE
        n = q0 + Q_TILE
        qb = (q_ref[0, q0:n, :] * ATTN_SCALE).astype(BF16)
        s_diag = _causal(_dot_nt(qb, kb_ref[q0:n, :]) - fr_ref[0, :, q0:n], 0)
        parts = [(s_diag, vb_ref[q0:n, :]), (_dot_nt(qb, kmb) - f_meta, vmb)]
        if q0 > 0:
            parts.append((_dot_nt(qb, kb_ref[:q0, :]) - fr_ref[0, :, :q0], vb_ref[:q0, :]))
        o_ref[q0:n, :] = _softmax_pv(parts)


def _decode_step(pt_ref, q_ref, kn_ref, vn_ref, lfn_ref, ck_ref, cv_ref, clf_ref, o_ref,
                 kbuf, vbuf, lfbuf, f_ref, sems, *, layer, n_pages):
    b = pl.program_id(0)
    nb = pl.num_programs(0)
    half = n_pages // 2
    n_heads = q_ref.shape[1]
    keys_half = half * PAGE_SIZE * n_heads
    n_chunks = keys_half // LANE
    slot = b % 2

    def copies(seq, slot_):
        out = []
        for j in range(n_pages):
            page = pt_ref[seq * n_pages + j]
            dst_lanes = pl.ds((j // half) * HEAD_DIM, HEAD_DIM)
            out.append(pltpu.make_async_copy(
                ck_ref.at[layer, page], kbuf.at[slot_, j % half, :, :, dst_lanes], sems.at[0, slot_]))
            out.append(pltpu.make_async_copy(
                cv_ref.at[layer, page], vbuf.at[slot_, j % half, :, :, dst_lanes], sems.at[1, slot_]))
            out.append(pltpu.make_async_copy(
                clf_ref.at[layer, page], lfbuf.at[slot_, j], sems.at[2, slot_]))
        return out

    @pl.when(b == 0)
    def _():
        for c in copies(0, 0):
            c.start()

    @pl.when(b + 1 < nb)
    def _():
        for c in copies(b + 1, 1 - slot):
            c.start()

    for c in copies(b, slot):
        c.wait()

    x = lfbuf[slot].reshape(n_pages * SUBLANE, LANE)
    nr = x.shape[0]
    w_in_row = _ones_where((LANE, LANE), lambda r, c: (r % n_heads == c % n_heads) & (r <= c))
    w_row_tot = _ones_where((LANE, LANE), lambda r, c: r % n_heads == c % n_heads)
    l_strict = _ones_where((nr, nr), lambda r, c: c < r)
    f_all = _dot3(x, w_in_row) + _dot3_rhs(l_strict, _dot3(x, w_row_tot))
    f_ref[...] = f_all

    q = q_ref[0] * ATTN_SCALE
    zero = jnp.zeros_like(q)
    q2 = jnp.concatenate([jnp.concatenate([q, zero], axis=1),
                          jnp.concatenate([zero, q], axis=1)], axis=0)
    kmat = kbuf[slot].reshape(keys_half, 2 * HEAD_DIM)
    s2 = _dot_nt(q2, kmat)

    f_lo = jnp.concatenate(
        [jnp.broadcast_to(f_ref[c:c + 1, :], (n_heads, LANE)) for c in range(n_chunks)], axis=1)
    f_hi = jnp.concatenate(
        [jnp.broadcast_to(f_ref[n_chunks + c:n_chunks + c + 1, :], (n_heads, LANE)) for c in range(n_chunks)], axis=1)
    f2 = jnp.concatenate([f_lo, f_hi], axis=0)

    row = lax.broadcasted_iota(jnp.int32, s2.shape, 0)
    col = lax.broadcasted_iota(jnp.int32, s2.shape, 1)
    logit = jnp.where(col % n_heads == row % n_heads, s2 - f2, -jnp.inf)
    m2 = jnp.max(logit, axis=-1, keepdims=True)
    m = jnp.maximum(m2[:n_heads], m2[n_heads:])
    p = jnp.exp(logit - jnp.concatenate([m, m], axis=0))
    l2 = jnp.sum(p, axis=-1, keepdims=True)
    l = l2[:n_heads] + l2[n_heads:]
    vmat = vbuf[slot].reshape(keys_half, 2 * HEAD_DIM)
    o2 = jnp.dot(p, vmat, preferred_element_type=F32)
    o = o2[:n_heads, :HEAD_DIM] + o2[n_heads:, HEAD_DIM:]

    last = jnp.broadcast_to(f_ref[nr - 1:nr, :], (n_heads, LANE))
    r8 = lax.broadcasted_iota(jnp.int32, (n_heads, LANE), 0)
    c8 = lax.broadcasted_iota(jnp.int32, (n_heads, LANE), 1)
    f_past = jnp.sum(jnp.where(c8 == LANE - n_heads + r8, last, 0.0), axis=-1, keepdims=True)
    s_new = jnp.sum(q * kn_ref[0], axis=-1, keepdims=True)
    logit_new = s_new - (f_past + lfn_ref[0])
    m_fin = jnp.maximum(m, logit_new)
    alpha = jnp.exp(m - m_fin)
    p_new = jnp.exp(logit_new - m_fin)
    o_ref[0] = (alpha * o + p_new * vn_ref[0]) / (alpha * l + p_new)


def _mixed_attn_kernel(pt_ref, q_ref, kn_ref, vn_ref, lfn_ref, ck_ref, cv_ref, clf_ref,
                       pq_ref, pk_ref, pv_ref, pqm_ref, pkm_ref, pvm_ref, fr_ref, fm_ref,
                       o_ref, po_ref, pom_ref, kbuf, vbuf, lfbuf, f_ref, sems, pkb_ref, pvb_ref,
                       *, layer, n_pages, per):
    _decode_step(pt_ref, q_ref, kn_ref, vn_ref, lfn_ref, ck_ref, cv_ref, clf_ref, o_ref,
                 kbuf, vbuf, lfbuf, f_ref, sems, layer=layer, n_pages=n_pages)
    n_tiles = pq_ref.shape[1] // Q_TILE
    part = pl.program_id(0) % per
    for s_ in range(per):
        tiles = (s_, n_tiles - 1 - s_)

        @pl.when(part == s_)
        def _():
            _prompt_attn_tiles(pq_ref, pk_ref, pv_ref, pqm_ref, pkm_ref, pvm_ref, fr_ref, fm_ref,
                               po_ref, pom_ref, pkb_ref, pvb_ref, tiles, first=(s_ == 0))


def _mixed_attention(page_table, q, k_new, v_new, lf_new, cache_k, cache_v, cache_lf, layer,
                     z4, f_real, f_meta, batch, seq, meta_row0):
    n_seq, n_pages = page_table.shape
    _, n_heads, head_dim = q.shape
    half = n_pages // 2
    units = batch * n_heads
    per = n_seq // units
    assert n_pages % 2 == 0 and PAGE_SIZE * n_heads == SUBLANE * LANE and head_dim == HEAD_DIM
    assert n_seq == per * units and seq == 2 * per * Q_TILE
    meta_blk0 = meta_row0 // N_META

    def row_map(b, pt):
        return (b, 0, 0)

    def zspec(which, meta):
        if meta:
            return pl.BlockSpec((1, N_META, HEAD_DIM),
                                lambda b, pt: (which, meta_blk0 + (b // per) // n_heads, (b // per) % n_heads))
        return pl.BlockSpec((1, seq, HEAD_DIM), lambda b, pt: (which, (b // per) // n_heads, (b // per) % n_heads))

    def unit_map(b, pt):
        return ((b // per) // n_heads, (b // per) % n_heads)

    any_spec = pl.BlockSpec(memory_space=pl.ANY)
    grid_spec = pltpu.PrefetchScalarGridSpec(
        num_scalar_prefetch=1,
        grid=(n_seq,),
        in_specs=[
            pl.BlockSpec((1, n_heads, head_dim), row_map),
            pl.BlockSpec((1, n_heads, head_dim), row_map),
            pl.BlockSpec((1, n_heads, head_dim), row_map),
            pl.BlockSpec((1, n_heads, 1), row_map),
            any_spec, any_spec, any_spec,
            zspec(0, False), zspec(1, False), zspec(2, False),
            zspec(0, True), zspec(1, True), zspec(2, True),
            pl.BlockSpec((1, 1, seq), lambda b, pt: (b // per, 0, 0)),
            pl.BlockSpec((1, 1, LANE), lambda b, pt: (b // per, 0, 0)),
        ],
        out_specs=[
            pl.BlockSpec((1, n_heads, head_dim), row_map),
            pl.BlockSpec((seq, HEAD_DIM), unit_map),
            pl.BlockSpec((N_META, HEAD_DIM), unit_map),
        ],
        scratch_shapes=[
            pltpu.VMEM((2, half, PAGE_SIZE, n_heads, 2 * head_dim), F32),
            pltpu.VMEM((2, half, PAGE_SIZE, n_heads, 2 * head_dim), F32),
            pltpu.VMEM((2, n_pages, SUBLANE, LANE), F32),
            pltpu.VMEM((n_pages * SUBLANE, LANE), F32),
            pltpu.SemaphoreType.DMA((3, 2)),
            pltpu.VMEM((seq, HEAD_DIM), BF16),
            pltpu.VMEM((seq, HEAD_DIM), BF16),
        ],
    )
    return pl.pallas_call(
        functools.partial(_mixed_attn_kernel, layer=layer, n_pages=n_pages, per=per),
        grid_spec=grid_spec,
        out_shape=[
            jax.ShapeDtypeStruct((n_seq, n_heads, head_dim), F32),
            jax.ShapeDtypeStruct((batch * seq, n_heads * HEAD_DIM), F32),
            jax.ShapeDtypeStruct((batch * N_META, n_heads * HEAD_DIM), F32),
        ],
        compiler_params=pltpu.CompilerParams(
            dimension_semantics=("arbitrary",), vmem_limit_bytes=VMEM_LIMIT),
        name="mixed_attn",
    )(page_table.reshape(-1), q, k_new, v_new, lf_new, cache_k, cache_v, cache_lf,
      z4, z4, z4, z4, z4, z4, f_real, f_meta)


def _pool_project(d_groups, wp_ref, sc_ref, o_ref):
    for g, d in enumerate(d_groups):
        gw = d.shape[1]
        cs = slice(g * gw, (g + 1) * gw)
        o_ref[:, cs] = jnp.dot(d.astype(BF16), wp_ref[g], preferred_element_type=F32) * sc_ref[:, cs]


def _pool_real_kernel(halo_ref, cur_ref, wp_ref, sc_ref, o_ref, ext_ref):
    rows = cur_ref.shape[0]
    gw = cur_ref.shape[1] // len(POOL_WINDOWS)
    ext_ref[0:N_META, :] = halo_ref[...]
    ext_ref[N_META:, :] = cur_ref[...]
    ds = []
    for g, w in enumerate(POOL_WINDOWS):
        cs = slice(g * gw, (g + 1) * gw)
        cur = ext_ref[N_META:N_META + rows, cs]
        s = cur
        for back in range(1, w):
            s = s + ext_ref[N_META - back:N_META - back + rows, cs]
        ds.append(s * (1.0 / w) - cur)
    _pool_project(ds, wp_ref, sc_ref, o_ref)


def _pool_real(z4, w_pool, scale, batch, seq, meta_row0):
    width = z4.shape[2]
    tiles = seq // POOL_TILE
    per16 = POOL_TILE // N_META
    meta_blk0 = meta_row0 // N_META

    def halo_map(b, i):
        return (3, jnp.where(i == 0, meta_blk0 + b, (b * tiles + i) * per16 - 1), 0)

    return pl.pallas_call(
        _pool_real_kernel,
        grid=(batch, tiles),
        in_specs=[
            pl.BlockSpec((None, N_META, width), halo_map),
            pl.BlockSpec((None, POOL_TILE, width), lambda b, i: (3, b * tiles + i, 0)),
            pl.BlockSpec(w_pool.shape, lambda b, i: (0, 0, 0)),
            pl.BlockSpec((1, width), lambda b, i: (0, 0)),
        ],
        out_specs=pl.BlockSpec((POOL_TILE, width), lambda b, i: (b * tiles + i, 0)),
        out_shape=jax.ShapeDtypeStruct((batch * seq, width), F32),
        scratch_shapes=[pltpu.VMEM((N_META + POOL_TILE, width), F32)],
        compiler_params=pltpu.CompilerParams(
            dimension_semantics=("parallel", "parallel"), vmem_limit_bytes=VMEM_LIMIT),
        name="pool_real",
    )(z4, z4, w_pool, scale)


def _pool_small_kernel(pm_ref, ps_ref, st_ref, wp_ref, sc_ref, om_ref, os_ref, ext_ref):
    width = pm_ref.shape[1]
    gw = width // len(POOL_WINDOWS)
    ext_ref[0:N_META, :] = jnp.zeros((N_META, width), F32)
    pos = lax.broadcasted_iota(jnp.int32, (N_META, 1), 0)
    for b in range(pm_ref.shape[0] // N_META):
        ext_ref[N_META:, :] = pm_ref[b * N_META:(b + 1) * N_META, :]
        ds = []
        for g, w in enumerate(POOL_WINDOWS):
            cs = slice(g * gw, (g + 1) * gw)
            cur = ext_ref[N_META:2 * N_META, cs]
            s = cur
            for back in range(1, w):
                s = s + ext_ref[N_META - back:2 * N_META - back, cs]
            cnt = jnp.minimum(pos + 1, w).astype(F32)
            ds.append(s / cnt - cur)
        _pool_project(ds, wp_ref, sc_ref, om_ref.at[b * N_META:(b + 1) * N_META, :])
    ds = []
    for g, w in enumerate(POOL_WINDOWS):
        cs = slice(g * gw, (g + 1) * gw)
        cur = ps_ref[:, cs]
        s = cur
        for back in range(1, w):
            s = s + st_ref[POOL_CTX - back, :, cs]
        ds.append(s * (1.0 / w) - cur)
    _pool_project(ds, wp_ref, sc_ref, os_ref)


def _pool_small(p_meta, p_sample, state_t, w_pool, scale):
    return pl.pallas_call(
        _pool_small_kernel,
        out_shape=[jax.ShapeDtypeStruct(p_meta.shape, F32), jax.ShapeDtypeStruct(p_sample.shape, F32)],
        scratch_shapes=[pltpu.VMEM((2 * N_META, p_meta.shape[1]), F32)],
        compiler_params=pltpu.CompilerParams(vmem_limit_bytes=VMEM_LIMIT),
        name="pool_small",
    )(p_meta, p_sample, state_t, w_pool, scale)


def _outproj_kernel(h_ref, a_ref, pm_ref, at_ref, pmt_ref, w_ref, g_ref, o_ref):
    i = pl.program_id(0)
    n_real_tiles = pl.num_programs(0) - 1

    def project(a_src, pm_src):
        aw = a_src.shape[1]
        mix = jnp.dot(a_src[...].astype(BF16), w_ref[0:aw, :], preferred_element_type=F32)
        mix = mix + jnp.dot(pm_src[...].astype(BF16), w_ref[aw:, :], preferred_element_type=F32)
        o_ref[...] = h_ref[...] + _rms(mix, g_ref[...])

    @pl.when(i < n_real_tiles)
    def _():
        project(a_ref, pm_ref)

    @pl.when(i == n_real_tiles)
    def _():
        project(at_ref, pmt_ref)


def _out_proj(h, a_real, pm_real, a_tail, pm_tail, w_out, g):
    rows, d = h.shape
    n_real = a_real.shape[0]
    tile = rows - n_real
    assert a_tail.shape[0] == tile and n_real % tile == 0
    n_real_tiles = n_real // tile

    def real_map(i):
        return (jnp.minimum(i, n_real_tiles - 1), 0)

    return pl.pallas_call(
        _outproj_kernel,
        grid=(n_real_tiles + 1,),
        in_specs=[
            pl.BlockSpec((tile, d), lambda i: (i, 0)),
            pl.BlockSpec((tile, a_real.shape[1]), real_map),
            pl.BlockSpec((tile, pm_real.shape[1]), real_map),
            pl.BlockSpec((tile, a_tail.shape[1]), lambda i: (0, 0)),
            pl.BlockSpec((tile, pm_tail.shape[1]), lambda i: (0, 0)),
            pl.BlockSpec(w_out.shape, lambda i: (0, 0)),
            pl.BlockSpec((1, d), lambda i: (0, 0)),
        ],
        out_specs=pl.BlockSpec((tile, d), lambda i: (i, 0)),
        out_shape=jax.ShapeDtypeStruct((rows, d), F32),
        compiler_params=pltpu.CompilerParams(
            dimension_semantics=("arbitrary",), vmem_limit_bytes=VMEM_LIMIT),
        name="out_proj",
    )(h, a_real, pm_real, a_tail, pm_tail, w_out, g)


def _kv_assemble_kernel(*refs, depth, n_heads):
    z_refs = refs[:2 * depth]
    k_ref, v_ref = refs[2 * depth:]
    layer = pl.program_id(0)
    tensor = pl.program_id(2)

    def retile(real_ref, meta_ref, o_ref):
        for h in range(n_heads):
            lanes = slice(h * HEAD_DIM, (h + 1) * HEAD_DIM)
            o_ref[0:N_META, h, :] = meta_ref[:, lanes]
            o_ref[N_META:, h, :] = real_ref[:, lanes]

    for l in range(depth):
        for t, o_ref in enumerate((k_ref, v_ref)):
            @pl.when((layer == l) & (tensor == t))
            def _():
                retile(z_refs[2 * l], z_refs[2 * l + 1], o_ref)


def _kv_assemble(z4s, batch, seq, n_heads, meta_row0):
    depth = len(z4s)
    width = z4s[0].shape[2]
    meta_blk0 = meta_row0 // N_META
    shape = jax.ShapeDtypeStruct((depth, batch, N_META + seq, n_heads, HEAD_DIM), F32)

    in_specs = []
    args = []
    for l, z in enumerate(z4s):
        def pick(l_, b, t, l=l):
            bb = jnp.where(l_ == l, b, jnp.where(l_ < l, 0, batch - 1))
            tt = jnp.where(l_ == l, t, jnp.where(l_ < l, 0, 1))
            return bb, tt

        def real_map(l_, b, t, pick=pick):
            bb, tt = pick(l_, b, t)
            return (1 + tt, bb, 0)

        def meta_map(l_, b, t, pick=pick):
            bb, tt = pick(l_, b, t)
            return (1 + tt, meta_blk0 + bb, 0)

        in_specs.append(pl.BlockSpec((None, seq, width), real_map, pipeline_mode=pl.Buffered(1)))
        in_specs.append(pl.BlockSpec((None, N_META, width), meta_map))
        args += [z, z]

    out_spec = pl.BlockSpec((None, None, N_META + seq, n_heads, HEAD_DIM), lambda l_, b, t: (l_, b, 0, 0, 0))
    return pl.pallas_call(
        functools.partial(_kv_assemble_kernel, depth=depth, n_heads=n_heads),
        grid=(depth, batch, 2),
        in_specs=in_specs,
        out_specs=[out_spec, out_spec],
        out_shape=[shape, shape],
        compiler_params=pltpu.CompilerParams(
            dimension_semantics=("arbitrary", "arbitrary", "arbitrary"), vmem_limit_bytes=VMEM_LIMIT),
        name="kv_assemble",
    )(*args)


def kernel(x_prompt, x_sample, cache_k, cache_v, cache_logf, state_pool, page_table, meta_tokens, w_in, b_forget, w_pool, pool_scale, w_out, ffn1_gate, ffn1_up, ffn1_down, ffn2_gate, ffn2_up, ffn2_down, g_pre_ffn1, g_post_ffn1, g_pre_mix, g_post_mix, g_pre_ffn2, g_post_ffn2):
    batch, seq, d_model = x_prompt.shape
    n_seq = x_sample.shape[0]
    depth, n_phys, page, n_heads, head_dim = cache_k.shape
    attn_w = n_heads * head_dim
    pool_w = state_pool.shape[-1]
    assert head_dim == HEAD_DIM and page == PAGE_SIZE and x_sample.shape[1] == 1
    assert attn_w == pool_w and w_in.shape[-1] == 3 * attn_w + pool_w + n_heads

    n_real = batch * seq
    s0 = n_real
    m0 = s0 + n_seq
    n_used = m0 + batch * N_META
    assert n_used <= ROW_ALIGN and m0 % N_META == 0 and seq % POOL_TILE == 0 and seq % Q_TILE == 0

    h = jnp.concatenate([
        x_prompt.reshape(n_real, d_model),
        x_sample.reshape(n_seq, d_model),
        jnp.tile(meta_tokens.astype(F32), (batch, 1)),
        jnp.zeros((ROW_ALIGN - n_used, d_model), F32)], axis=0)

    clf = cache_logf.reshape(depth, n_phys, SUBLANE, LANE)
    w_in_t = jnp.swapaxes(w_in, 1, 2)
    pad_rows = jnp.zeros((ROW_ALIGN - n_used, attn_w), F32)

    def row(v):
        return v.reshape(1, -1)

    outs = {k: [] for k in ("lfp", "pp", "ks", "vs", "lfs", "ps")}
    z4s = []
    for l in range(depth):
        w_f_t = jnp.pad(w_in_t[l, 3 * attn_w + pool_w:, :], ((0, LANE - n_heads), (0, 0)))
        b_f = jnp.pad(b_forget[l].astype(F32), (0, LANE - n_heads)).reshape(1, LANE)

        h = _ffn_half(h, row(g_pre_ffn1[l]), row(g_post_ffn1[l]), ffn1_gate, ffn1_up, ffn1_down, l)

        z4, lf = _in_proj(h, row(g_pre_mix[l]), w_in_t, w_f_t, b_f, n_heads, l)

        lf_real_t = lf[:n_real].reshape(batch, seq, n_heads).transpose(0, 2, 1).reshape(batch * n_heads, seq)
        lf_meta_t = lf[m0:n_used].reshape(batch, N_META, n_heads).transpose(0, 2, 1).reshape(batch * n_heads, N_META)
        lf_meta_t = jnp.pad(lf_meta_t, ((0, 0), (0, LANE - N_META)))
        f_meta, f_real = _prompt_fcum(lf_meta_t, lf_real_t)
        q_s = z4[0, s0:m0].reshape(n_seq, n_heads, head_dim)
        k_s = z4[1, s0:m0].reshape(n_seq, n_heads, head_dim)
        v_s = z4[2, s0:m0].reshape(n_seq, n_heads, head_dim)
        p_s = z4[3, s0:m0]
        lf_s = lf[s0:m0]
        a_s, a_real, a_meta = _mixed_attention(
            page_table, q_s, k_s, v_s, lf_s.reshape(n_seq, n_heads, 1), cache_k, cache_v, clf, l,
            z4, f_real.reshape(batch * n_heads, 1, seq), f_meta.reshape(batch * n_heads, 1, LANE),
            batch, seq, m0)
        a_s = a_s.reshape(n_seq, attn_w)

        wp = w_pool[l].astype(BF16)
        sc = row(pool_scale[l].astype(F32))
        pm_real = _pool_real(z4, wp, sc, batch, seq, m0)
        pm_meta, pm_s = _pool_small(z4[3, m0:n_used], p_s, state_pool[l].transpose(1, 0, 2), wp, sc)

        a_tail = jnp.concatenate([a_s, a_meta, pad_rows], axis=0)
        pm_tail = jnp.concatenate([pm_s, pm_meta, pad_rows], axis=0)
        h = _out_proj(h, a_real, pm_real, a_tail, pm_tail, w_out[l].astype(BF16), row(g_post_mix[l]))

        h = _ffn_half(h, row(g_pre_ffn2[l]), row(g_post_ffn2[l]), ffn2_gate, ffn2_up, ffn2_down, l)

        def seq_major(x_real, x_meta, tail):
            return jnp.concatenate(
                [x_meta.reshape((batch, N_META) + tail), x_real.reshape((batch, seq) + tail)], axis=1)

        z4s.append(z4)
        outs["lfp"].append(seq_major(lf[:n_real], lf[m0:n_used], (n_heads,)))
        outs["pp"].append(jnp.stack([z4[3, (b + 1) * seq - POOL_CTX:(b + 1) * seq] for b in range(batch)], 0))
        outs["ks"].append(k_s.reshape(n_seq, 1, n_heads, head_dim))
        outs["vs"].append(v_s.reshape(n_seq, 1, n_heads, head_dim))
        outs["lfs"].append(lf_s.reshape(n_seq, 1, n_heads))
        outs["ps"].append(jnp.concatenate([state_pool[l][:, 1:], p_s[:, None, :]], axis=1))

    y_prompt = h[:n_real].reshape(batch, seq, d_model)
    y_sample = h[s0:m0].reshape(n_seq, 1, d_model)
    st = {k: jnp.stack(v, 0) for k, v in outs.items()}
    k_prompt, v_prompt = _kv_assemble(z4s, batch, seq, n_heads, m0)
    return (y_prompt, y_sample, k_prompt, v_prompt, st["lfp"], st["pp"],
            st["ks"], st["vs"], st["lfs"], st["ps"])
```

```python
import functools

import jax
import jax.numpy as jnp
from jax import lax
from jax.experimental import pallas as pl
from jax.experimental.pallas import tpu as pltpu

F32 = jnp.float32
BF16 = jnp.bfloat16

N_META = 16
HEAD_DIM = 128
PAGE_SIZE = 128
POOL_WINDOWS = (2, 4, 8, 16)
POOL_CTX = max(POOL_WINDOWS) - 1
RMS_EPS = 1e-6
NEG_INF = -1e30
ATTN_SCALE = HEAD_DIM ** -0.5

LANE = 128
SUBLANE = 8
ROW_TILE = 768
INPROJ_TILE = 1056
ROW_ALIGN = 8448
FF_TILE = 512
POOL_TILE = 512
Q_TILE = 256
VMEM_LIMIT = 56 * 1024 * 1024
FFN_VMEM_LIMIT = 60 * 1024 * 1024


def _rms(x, g):
    return x * lax.rsqrt(jnp.mean(x * x, axis=-1, keepdims=True) + RMS_EPS) * g


def _split3(x):
    hi = x.astype(BF16)
    r1 = x - hi.astype(F32)
    mid = r1.astype(BF16)
    lo = (r1 - mid.astype(F32)).astype(BF16)
    return hi, mid, lo


def _dot3(x, w01):
    hi, mid, lo = _split3(x)
    acc = jnp.dot(hi, w01, preferred_element_type=F32)
    acc = acc + jnp.dot(mid, w01, preferred_element_type=F32)
    return acc + jnp.dot(lo, w01, preferred_element_type=F32)


def _dot3_rhs(w01, x):
    hi, mid, lo = _split3(x)
    acc = jnp.dot(w01, hi, preferred_element_type=F32)
    acc = acc + jnp.dot(w01, mid, preferred_element_type=F32)
    return acc + jnp.dot(w01, lo, preferred_element_type=F32)


def _dot_nt(a, b):
    return lax.dot_general(a, b, (((1,), (1,)), ((), ())), preferred_element_type=F32)


def _dot_mixed(a, b):
    return lax.dot_general(a, b, (((1,), (0,)), ((), ())), preferred_element_type=F32)


def _ffn_kernel(x_ref, gpre_ref, gpost_ref, wg_ref, wu_ref, wd_ref, o_ref, xn_ref, *, last_valid):
    j = pl.program_id(1)
    last = pl.num_programs(1) - 1

    def swiglu_part(cols):
        xn = xn_ref[...]
        a = _dot_mixed(xn, wg_ref[:, :cols])
        b = _dot_mixed(xn, wu_ref[:, :cols])
        return _dot_mixed((a * jax.nn.sigmoid(a) * b).astype(BF16), wd_ref[:cols, :])

    @pl.when(j == 0)
    def _():
        xn_ref[...] = _rms(x_ref[...], gpre_ref[...]).astype(BF16)
        o_ref[...] = swiglu_part(wg_ref.shape[1])

    @pl.when((j > 0) & (j < last))
    def _():
        o_ref[...] += swiglu_part(wg_ref.shape[1])

    @pl.when(j == last)
    def _():
        y = o_ref[...] + swiglu_part(last_valid)
        o_ref[...] = x_ref[...] + 0.5 * _rms(y, gpost_ref[...])


def _ffn_half(h, g_pre, g_post, wg, wu, wd, layer):
    rows, d = h.shape
    ff = wg.shape[2]
    n_blocks = pl.cdiv(ff, FF_TILE)
    last_valid = ff - (n_blocks - 1) * FF_TILE
    assert last_valid % LANE == 0
    return pl.pallas_call(
        functools.partial(_ffn_kernel, last_valid=last_valid),
        grid=(rows // ROW_TILE, n_blocks),
        in_specs=[
            pl.BlockSpec((ROW_TILE, d), lambda i, j: (i, 0)),
            pl.BlockSpec((1, d), lambda i, j: (0, 0)),
            pl.BlockSpec((1, d), lambda i, j: (0, 0)),
            pl.BlockSpec((None, d, FF_TILE), lambda i, j: (layer, 0, j)),
            pl.BlockSpec((None, d, FF_TILE), lambda i, j: (layer, 0, j)),
            pl.BlockSpec((None, FF_TILE, d), lambda i, j: (layer, j, 0)),
        ],
        out_specs=pl.BlockSpec((ROW_TILE, d), lambda i, j: (i, 0)),
        out_shape=jax.ShapeDtypeStruct((rows, d), F32),
        scratch_shapes=[pltpu.VMEM((ROW_TILE, d), BF16)],
        compiler_params=pltpu.CompilerParams(
            dimension_semantics=("parallel", "arbitrary"), vmem_limit_bytes=FFN_VMEM_LIMIT),
        name="ffn_half",
    )(h, g_pre, g_post, wg, wu, wd)


def _inproj_kernel(h_ref, g_ref, w_ref, wf_ref, bf_ref, z_ref, lf_ref, u_ref):
    n = pl.program_id(1)

    @pl.when(n == 0)
    def _():
        u = _rms(h_ref[...], g_ref[...]).astype(BF16)
        u_ref[...] = u
        x = _dot_nt(u, wf_ref[...]) + bf_ref[...]
        lf = jnp.minimum(x, 0.0) - jnp.log1p(jnp.exp(-jnp.abs(x)))
        lf_ref[...] = lf[:, :lf_ref.shape[1]]

    z_ref[0] = _dot_nt(u_ref[...], w_ref[...])


def _in_proj(h, g, w_in_t, w_f_t, b_f, n_heads, layer):
    rows, d = h.shape
    col_tile = (w_in_t.shape[1] - n_heads) // 4
    return pl.pallas_call(
        _inproj_kernel,
        grid=(rows // INPROJ_TILE, 4),
        in_specs=[
            pl.BlockSpec((INPROJ_TILE, d), lambda i, n: (i, 0)),
            pl.BlockSpec((1, d), lambda i, n: (0, 0)),
            pl.BlockSpec((None, col_tile, d), lambda i, n: (layer, n, 0)),
            pl.BlockSpec((LANE, d), lambda i, n: (0, 0)),
            pl.BlockSpec((1, LANE), lambda i, n: (0, 0)),
        ],
        out_specs=[
            pl.BlockSpec((1, INPROJ_TILE, col_tile), lambda i, n: (n, i, 0)),
            pl.BlockSpec((INPROJ_TILE, n_heads), lambda i, n: (i, 0)),
        ],
        out_shape=[
            jax.ShapeDtypeStruct((4, rows, col_tile), F32),
            jax.ShapeDtypeStruct((rows, n_heads), F32),
        ],
        scratch_shapes=[pltpu.VMEM((INPROJ_TILE, d), BF16)],
        compiler_params=pltpu.CompilerParams(
            dimension_semantics=("parallel", "arbitrary"), vmem_limit_bytes=VMEM_LIMIT),
        name="in_proj",
    )(h, g, w_in_t, w_f_t, b_f)


def _ones_where(shape, pred):
    r = lax.broadcasted_iota(jnp.int32, shape, 0)
    c = lax.broadcasted_iota(jnp.int32, shape, 1)
    return jnp.where(pred(r, c), 1.0, 0.0).astype(BF16)


def _fcum_kernel(lfm_ref, lfr_ref, fm_ref, fr_ref):
    u = _ones_where((LANE, LANE), lambda r, c: r <= c)
    fm = _dot3(lfm_ref[...], u)
    fm_ref[...] = fm
    carry = fm[:, LANE - 1:LANE]
    for blk in range(lfr_ref.shape[1] // LANE):
        sl = slice(blk * LANE, (blk + 1) * LANE)
        f = _dot3(lfr_ref[:, sl], u) + carry
        fr_ref[:, sl] = f
        carry = f[:, LANE - 1:LANE]


def _prompt_fcum(lf_meta_t, lf_real_t):
    return pl.pallas_call(
        _fcum_kernel,
        out_shape=[jax.ShapeDtypeStruct(lf_meta_t.shape, F32), jax.ShapeDtypeStruct(lf_real_t.shape, F32)],
        name="prompt_fcum",
    )(lf_meta_t, lf_real_t)


def _softmax_pv(parts):
    m = None
    for s, _ in parts:
        mi = jnp.max(s, axis=-1, keepdims=True)
        m = mi if m is None else jnp.maximum(m, mi)
    l = None
    o = None
    for s, v in parts:
        p = jnp.exp(s - m)
        li = jnp.sum(p, axis=-1, keepdims=True)
        oi = jnp.dot(p.astype(BF16), v, preferred_element_type=F32)
        l = li if l is None else l + li
        o = oi if o is None else o + oi
    return o / l


def _causal(s, q0):
    r = lax.broadcasted_iota(jnp.int32, s.shape, 0) + q0
    c = lax.broadcasted_iota(jnp.int32, s.shape, 1)
    return jnp.where(c <= r, s, NEG_INF)


def _prompt_attn_tiles(q_ref, k_ref, v_ref, qm_ref, km_ref, vm_ref, fr_ref, fm_ref, o_ref, om_ref,
                        kb_ref, vb_ref, tiles, first):
    kmb = km_ref[0].astype(BF16)
    vmb = vm_ref[0].astype(BF16)
    f_meta = fm_ref[0][:, :N_META]

    if first:
        kb_ref[...] = k_ref[0].astype(BF16)
        vb_ref[...] = v_ref[0].astype(BF16)
        qmb = (qm_ref[0] * ATTN_SCALE).astype(BF16)
        sm = _causal(_dot_nt(qmb, kmb) - f_meta, 0)
        om_ref[...] = _softmax_pv([(sm, vmb)])

    for qi in tiles:
        q0 = qi * Q_TILE
        n = q0 + Q_TILE
        qb = (q_ref[0, q0:n, :] * ATTN_SCALE).astype(BF16)
        s_diag = _causal(_dot_nt(qb, kb_ref[q0:n, :]) - fr_ref[0, :, q0:n], 0)
        parts = [(s_diag, vb_ref[q0:n, :]), (_dot_nt(qb, kmb) - f_meta, vmb)]
        if q0 > 0:
            parts.append((_dot_nt(qb, kb_ref[:q0, :]) - fr_ref[0, :, :q0], vb_ref[:q0, :]))
        o_ref[q0:n, :] = _softmax_pv(parts)


def _decode_step(pt_ref, q_ref, kn_ref, vn_ref, lfn_ref, ck_ref, cv_ref, clf_ref, o_ref,
                 kbuf, vbuf, lfbuf, f_ref, sems, *, layer, n_pages):
    b = pl.program_id(0)
    nb = pl.num_programs(0)
    half = n_pages // 2
    n_heads = q_ref.shape[1]
    keys_half = half * PAGE_SIZE * n_heads
    n_chunks = keys_half // LANE
    slot = b % 2

    def copies(seq, slot_):
        out = []
        for j in range(n_pages):
            page = pt_ref[seq * n_pages + j]
            dst_lanes = pl.ds((j // half) * HEAD_DIM, HEAD_DIM)
            out.append(pltpu.make_async_copy(
                ck_ref.at[layer, page], kbuf.at[slot_, j % half, :, :, dst_lanes], sems.at[0, slot_]))
            out.append(pltpu.make_async_copy(
                cv_ref.at[layer, page], vbuf.at[slot_, j % half, :, :, dst_lanes], sems.at[1, slot_]))
            out.append(pltpu.make_async_copy(
                clf_ref.at[layer, page], lfbuf.at[slot_, j], sems.at[2, slot_]))
        return out

    @pl.when(b == 0)
    def _():
        for c in copies(0, 0):
            c.start()

    @pl.when(b + 1 < nb)
    def _():
        for c in copies(b + 1, 1 - slot):
            c.start()

    for c in copies(b, slot):
        c.wait()

    x = lfbuf[slot].reshape(n_pages * SUBLANE, LANE)
    nr = x.shape[0]
    w_in_row = _ones_where((LANE, LANE), lambda r, c: (r % n_heads == c % n_heads) & (r <= c))
    w_row_tot = _ones_where((LANE, LANE), lambda r, c: r % n_heads == c % n_heads)
    l_strict = _ones_where((nr, nr), lambda r, c: c < r)
    f_all = _dot3(x, w_in_row) + _dot3_rhs(l_strict, _dot3(x, w_row_tot))
    f_ref[...] = f_all

    q = q_ref[0] * ATTN_SCALE
    zero = jnp.zeros_like(q)
    q2 = jnp.concatenate([jnp.concatenate([q, zero], axis=1),
                          jnp.concatenate([zero, q], axis=1)], axis=0)
    kmat = kbuf[slot].reshape(keys_half, 2 * HEAD_DIM)
    s2 = _dot_nt(q2, kmat)

    f_lo = jnp.concatenate(
        [jnp.broadcast_to(f_ref[c:c + 1, :], (n_heads, LANE)) for c in range(n_chunks)], axis=1)
    f_hi = jnp.concatenate(
        [jnp.broadcast_to(f_ref[n_chunks + c:n_chunks + c + 1, :], (n_heads, LANE)) for c in range(n_chunks)], axis=1)
    f2 = jnp.concatenate([f_lo, f_hi], axis=0)

    row = lax.broadcasted_iota(jnp.int32, (n_heads, LANE), 0)
    col = lax.broadcasted_iota(jnp.int32, (n_heads, LANE), 1)
    bias = jnp.where(col % n_heads == row, 0.0, -jnp.inf)
    bias = jnp.concatenate([bias] * n_chunks, axis=1)
    bias = jnp.concatenate([bias, bias], axis=0)
    logit = s2 - f2 + bias
    m2 = jnp.max(logit, axis=-1, keepdims=True)
    m = jnp.maximum(m2[:n_heads], m2[n_heads:])
    p = jnp.exp(logit - jnp.concatenate([m, m], axis=0))
    l2 = jnp.sum(p, axis=-1, keepdims=True)
    l = l2[:n_heads] + l2[n_heads:]
    vmat = vbuf[slot].reshape(keys_half, 2 * HEAD_DIM)
    o2 = jnp.dot(p, vmat, preferred_element_type=F32)
    o = o2[:n_heads, :HEAD_DIM] + o2[n_heads:, HEAD_DIM:]

    last = jnp.broadcast_to(f_ref[nr - 1:nr, :], (n_heads, LANE))
    r8 = lax.broadcasted_iota(jnp.int32, (n_heads, LANE), 0)
    c8 = lax.broadcasted_iota(jnp.int32, (n_heads, LANE), 1)
    f_past = jnp.sum(jnp.where(c8 == LANE - n_heads + r8, last, 0.0), axis=-1, keepdims=True)
    s_new = jnp.sum(q * kn_ref[0], axis=-1, keepdims=True)
    logit_new = s_new - (f_past + lfn_ref[0])
    m_fin = jnp.maximum(m, logit_new)
    alpha = jnp.exp(m - m_fin)
    p_new = jnp.exp(logit_new - m_fin)
    o_ref[0] = (alpha * o + p_new * vn_ref[0]) / (alpha * l + p_new)


def _mixed_attn_kernel(pt_ref, q_ref, kn_ref, vn_ref, lfn_ref, ck_ref, cv_ref, clf_ref,
                       pq_ref, pk_ref, pv_ref, pqm_ref, pkm_ref, pvm_ref, fr_ref, fm_ref,
                       o_ref, po_ref, pom_ref, kbuf, vbuf, lfbuf, f_ref, sems, pkb_ref, pvb_ref,
                       *, layer, n_pages, per):
    _decode_step(pt_ref, q_ref, kn_ref, vn_ref, lfn_ref, ck_ref, cv_ref, clf_ref, o_ref,
                 kbuf, vbuf, lfbuf, f_ref, sems, layer=layer, n_pages=n_pages)
    n_tiles = pq_ref.shape[1] // Q_TILE
    part = pl.program_id(0) % per
    for s_ in range(per):
        tiles = (s_, n_tiles - 1 - s_)

        @pl.when(part == s_)
        def _():
            _prompt_attn_tiles(pq_ref, pk_ref, pv_ref, pqm_ref, pkm_ref, pvm_ref, fr_ref, fm_ref,
                               po_ref, pom_ref, pkb_ref, pvb_ref, tiles, first=(s_ == 0))


def _mixed_attention(page_table, q, k_new, v_new, lf_new, cache_k, cache_v, cache_lf, layer,
                     z4, f_real, f_meta, batch, seq, meta_row0):
    n_seq, n_pages = page_table.shape
    _, n_heads, head_dim = q.shape
    half = n_pages // 2
    units = batch * n_heads
    per = n_seq // units
    assert n_pages % 2 == 0 and PAGE_SIZE * n_heads == SUBLANE * LANE and head_dim == HEAD_DIM
    assert n_seq == per * units and seq == 2 * per * Q_TILE
    meta_blk0 = meta_row0 // N_META

    def row_map(b, pt):
        return (b, 0, 0)

    def zspec(which, meta):
        if meta:
            return pl.BlockSpec((1, N_META, HEAD_DIM),
                                lambda b, pt: (which, meta_blk0 + (b // per) // n_heads, (b // per) % n_heads))
        return pl.BlockSpec((1, seq, HEAD_DIM), lambda b, pt: (which, (b // per) // n_heads, (b // per) % n_heads))

    def unit_map(b, pt):
        return ((b // per) // n_heads, (b // per) % n_heads)

    any_spec = pl.BlockSpec(memory_space=pl.ANY)
    grid_spec = pltpu.PrefetchScalarGridSpec(
        num_scalar_prefetch=1,
        grid=(n_seq,),
        in_specs=[
            pl.BlockSpec((1, n_heads, head_dim), row_map),
            pl.BlockSpec((1, n_heads, head_dim), row_map),
            pl.BlockSpec((1, n_heads, head_dim), row_map),
            pl.BlockSpec((1, n_heads, 1), row_map),
            any_spec, any_spec, any_spec,
            zspec(0, False), zspec(1, False), zspec(2, False),
            zspec(0, True), zspec(1, True), zspec(2, True),
            pl.BlockSpec((1, 1, seq), lambda b, pt: (b // per, 0, 0)),
            pl.BlockSpec((1, 1, LANE), lambda b, pt: (b // per, 0, 0)),
        ],
        out_specs=[
            pl.BlockSpec((1, n_heads, head_dim), row_map),
            pl.BlockSpec((seq, HEAD_DIM), unit_map),
            pl.BlockSpec((N_META, HEAD_DIM), unit_map),
        ],
        scratch_shapes=[
            pltpu.VMEM((2, half, PAGE_SIZE, n_heads, 2 * head_dim), F32),
            pltpu.VMEM((2, half, PAGE_SIZE, n_heads, 2 * head_dim), F32),
            pltpu.VMEM((2, n_pages, SUBLANE, LANE), F32),
            pltpu.VMEM((n_pages * SUBLANE, LANE), F32),
            pltpu.SemaphoreType.DMA((3, 2)),
            pltpu.VMEM((seq, HEAD_DIM), BF16),
            pltpu.VMEM((seq, HEAD_DIM), BF16),
        ],
    )
    return pl.pallas_call(
        functools.partial(_mixed_attn_kernel, layer=layer, n_pages=n_pages, per=per),
        grid_spec=grid_spec,
        out_shape=[
            jax.ShapeDtypeStruct((n_seq, n_heads, head_dim), F32),
            jax.ShapeDtypeStruct((batch * seq, n_heads * HEAD_DIM), F32),
            jax.ShapeDtypeStruct((batch * N_META, n_heads * HEAD_DIM), F32),
        ],
        compiler_params=pltpu.CompilerParams(
            dimension_semantics=("arbitrary",), vmem_limit_bytes=VMEM_LIMIT),
        name="mixed_attn",
    )(page_table.reshape(-1), q, k_new, v_new, lf_new, cache_k, cache_v, cache_lf,
      z4, z4, z4, z4, z4, z4, f_real, f_meta)


def _pool_project(d_groups, wp_ref, sc_ref, o_ref):
    for g, d in enumerate(d_groups):
        gw = d.shape[1]
        cs = slice(g * gw, (g + 1) * gw)
        o_ref[:, cs] = jnp.dot(d.astype(BF16), wp_ref[g], preferred_element_type=F32) * sc_ref[:, cs]


def _pool_real_kernel(halo_ref, cur_ref, wp_ref, sc_ref, o_ref, ext_ref):
    rows = cur_ref.shape[0]
    gw = cur_ref.shape[1] // len(POOL_WINDOWS)
    ext_ref[0:N_META, :] = halo_ref[...]
    ext_ref[N_META:, :] = cur_ref[...]
    ds = []
    for g, w in enumerate(POOL_WINDOWS):
        cs = slice(g * gw, (g + 1) * gw)
        cur = ext_ref[N_META:N_META + rows, cs]
        s = cur
        for back in range(1, w):
            s = s + ext_ref[N_META - back:N_META - back + rows, cs]
        ds.append(s * (1.0 / w) - cur)
    _pool_project(ds, wp_ref, sc_ref, o_ref)


def _pool_real(z4, w_pool, scale, batch, seq, meta_row0):
    width = z4.shape[2]
    tiles = seq // POOL_TILE
    per16 = POOL_TILE // N_META
    meta_blk0 = meta_row0 // N_META

    def halo_map(b, i):
        return (3, jnp.where(i == 0, meta_blk0 + b, (b * tiles + i) * per16 - 1), 0)

    return pl.pallas_call(
        _pool_real_kernel,
        grid=(batch, tiles),
        in_specs=[
            pl.BlockSpec((None, N_META, width), halo_map),
            pl.BlockSpec((None, POOL_TILE, width), lambda b, i: (3, b * tiles + i, 0)),
            pl.BlockSpec(w_pool.shape, lambda b, i: (0, 0, 0)),
            pl.BlockSpec((1, width), lambda b, i: (0, 0)),
        ],
        out_specs=pl.BlockSpec((POOL_TILE, width), lambda b, i: (b * tiles + i, 0)),
        out_shape=jax.ShapeDtypeStruct((batch * seq, width), F32),
        scratch_shapes=[pltpu.VMEM((N_META + POOL_TILE, width), F32)],
        compiler_params=pltpu.CompilerParams(
            dimension_semantics=("parallel", "parallel"), vmem_limit_bytes=VMEM_LIMIT),
        name="pool_real",
    )(z4, z4, w_pool, scale)


def _pool_small_kernel(pm_ref, ps_ref, st_ref, wp_ref, sc_ref, om_ref, os_ref, ext_ref):
    width = pm_ref.shape[1]
    gw = width // len(POOL_WINDOWS)
    ext_ref[0:N_META, :] = jnp.zeros((N_META, width), F32)
    pos = lax.broadcasted_iota(jnp.int32, (N_META, 1), 0)
    for b in range(pm_ref.shape[0] // N_META):
        ext_ref[N_META:, :] = pm_ref[b * N_META:(b + 1) * N_META, :]
        ds = []
        for g, w in enumerate(POOL_WINDOWS):
            cs = slice(g * gw, (g + 1) * gw)
            cur = ext_ref[N_META:2 * N_META, cs]
            s = cur
            for back in range(1, w):
                s = s + ext_ref[N_META - back:2 * N_META - back, cs]
            cnt = jnp.minimum(pos + 1, w).astype(F32)
            ds.append(s / cnt - cur)
        _pool_project(ds, wp_ref, sc_ref, om_ref.at[b * N_META:(b + 1) * N_META, :])
    ds = []
    for g, w in enumerate(POOL_WINDOWS):
        cs = slice(g * gw, (g + 1) * gw)
        cur = ps_ref[:, cs]
        s = cur
        for back in range(1, w):
            s = s + st_ref[POOL_CTX - back, :, cs]
        ds.append(s * (1.0 / w) - cur)
    _pool_project(ds, wp_ref, sc_ref, os_ref)


def _pool_small(p_meta, p_sample, state_t, w_pool, scale):
    return pl.pallas_call(
        _pool_small_kernel,
        out_shape=[jax.ShapeDtypeStruct(p_meta.shape, F32), jax.ShapeDtypeStruct(p_sample.shape, F32)],
        scratch_shapes=[pltpu.VMEM((2 * N_META, p_meta.shape[1]), F32)],
        compiler_params=pltpu.CompilerParams(vmem_limit_bytes=VMEM_LIMIT),
        name="pool_small",
    )(p_meta, p_sample, state_t, w_pool, scale)


def _outproj_kernel(h_ref, a_ref, pm_ref, at_ref, pmt_ref, w_ref, g_ref, o_ref):
    i = pl.program_id(0)
    n_real_tiles = pl.num_programs(0) - 1

    def project(a_src, pm_src):
        aw = a_src.shape[1]
        mix = _dot_mixed(a_src[...].astype(BF16), w_ref[0:aw, :])
        mix = mix + _dot_mixed(pm_src[...].astype(BF16), w_ref[aw:, :])
        o_ref[...] = h_ref[...] + _rms(mix, g_ref[...])

    @pl.when(i < n_real_tiles)
    def _():
        project(a_ref, pm_ref)

    @pl.when(i == n_real_tiles)
    def _():
        project(at_ref, pmt_ref)


def _out_proj(h, a_real, pm_real, a_tail, pm_tail, w_out, g, layer):
    rows, d = h.shape
    n_real = a_real.shape[0]
    tile = rows - n_real
    assert a_tail.shape[0] == tile and n_real % tile == 0
    n_real_tiles = n_real // tile

    def real_map(i):
        return (jnp.minimum(i, n_real_tiles - 1), 0)

    return pl.pallas_call(
        _outproj_kernel,
        grid=(n_real_tiles + 1,),
        in_specs=[
            pl.BlockSpec((tile, d), lambda i: (i, 0)),
            pl.BlockSpec((tile, a_real.shape[1]), real_map),
            pl.BlockSpec((tile, pm_real.shape[1]), real_map),
            pl.BlockSpec((tile, a_tail.shape[1]), lambda i: (0, 0)),
            pl.BlockSpec((tile, pm_tail.shape[1]), lambda i: (0, 0)),
            pl.BlockSpec((None,) + w_out.shape[1:], lambda i: (layer, 0, 0), pipeline_mode=pl.Buffered(1)),
            pl.BlockSpec((1, d), lambda i: (0, 0)),
        ],
        out_specs=pl.BlockSpec((tile, d), lambda i: (i, 0)),
        out_shape=jax.ShapeDtypeStruct((rows, d), F32),
        compiler_params=pltpu.CompilerParams(
            dimension_semantics=("arbitrary",), vmem_limit_bytes=VMEM_LIMIT),
        name="out_proj",
    )(h, a_real, pm_real, a_tail, pm_tail, w_out, g)


def _kv_assemble_kernel(*refs, depth, n_heads):
    z_refs = refs[:2 * depth]
    k_ref, v_ref = refs[2 * depth:]
    layer = pl.program_id(0)
    tensor = pl.program_id(2)

    def retile(real_ref, meta_ref, o_ref):
        for h in range(n_heads):
            lanes = slice(h * HEAD_DIM, (h + 1) * HEAD_DIM)
            o_ref[0:N_META, h, :] = meta_ref[:, lanes]
            o_ref[N_META:, h, :] = real_ref[:, lanes]

    for l in range(depth):
        for t, o_ref in enumerate((k_ref, v_ref)):
            @pl.when((layer == l) & (tensor == t))
            def _():
                retile(z_refs[2 * l], z_refs[2 * l + 1], o_ref)


def _kv_assemble(z4s, batch, seq, n_heads, meta_row0):
    depth = len(z4s)
    width = z4s[0].shape[2]
    meta_blk0 = meta_row0 // N_META
    shape = jax.ShapeDtypeStruct((depth, batch, N_META + seq, n_heads, HEAD_DIM), F32)

    in_specs = []
    args = []
    for l, z in enumerate(z4s):
        def pick(l_, b, t, l=l):
            bb = jnp.where(l_ == l, b, jnp.where(l_ < l, 0, batch - 1))
            tt = jnp.where(l_ == l, t, jnp.where(l_ < l, 0, 1))
            return bb, tt

        def real_map(l_, b, t, pick=pick):
            bb, tt = pick(l_, b, t)
            return (1 + tt, bb, 0)

        def meta_map(l_, b, t, pick=pick):
            bb, tt = pick(l_, b, t)
            return (1 + tt, meta_blk0 + bb, 0)

        in_specs.append(pl.BlockSpec((None, seq, width), real_map, pipeline_mode=pl.Buffered(1)))
        in_specs.append(pl.BlockSpec((None, N_META, width), meta_map))
        args += [z, z]

    out_spec = pl.BlockSpec((None, None, N_META + seq, n_heads, HEAD_DIM), lambda l_, b, t: (l_, b, 0, 0, 0))
    return pl.pallas_call(
        functools.partial(_kv_assemble_kernel, depth=depth, n_heads=n_heads),
        grid=(depth, batch, 2),
        in_specs=in_specs,
        out_specs=[out_spec, out_spec],
        out_shape=[shape, shape],
        compiler_params=pltpu.CompilerParams(
            dimension_semantics=("arbitrary", "arbitrary", "arbitrary"), vmem_limit_bytes=VMEM_LIMIT),
        name="kv_assemble",
    )(*args)


def kernel(x_prompt, x_sample, cache_k, cache_v, cache_logf, state_pool, page_table, meta_tokens, w_in, b_forget, w_pool, pool_scale, w_out, ffn1_gate, ffn1_up, ffn1_down, ffn2_gate, ffn2_up, ffn2_down, g_pre_ffn1, g_post_ffn1, g_pre_mix, g_post_mix, g_pre_ffn2, g_post_ffn2):
    batch, seq, d_model = x_prompt.shape
    n_seq = x_sample.shape[0]
    depth, n_phys, page, n_heads, head_dim = cache_k.shape
    attn_w = n_heads * head_dim
    pool_w = state_pool.shape[-1]
    assert head_dim == HEAD_DIM and page == PAGE_SIZE and x_sample.shape[1] == 1
    assert attn_w == pool_w and w_in.shape[-1] == 3 * attn_w + pool_w + n_heads

    n_real = batch * seq
    s0 = n_real
    m0 = s0 + n_seq
    n_used = m0 + batch * N_META
    assert n_used <= ROW_ALIGN and m0 % N_META == 0 and seq % POOL_TILE == 0 and seq % Q_TILE == 0

    h = jnp.concatenate([
        x_prompt.reshape(n_real, d_model),
        x_sample.reshape(n_seq, d_model),
        jnp.tile(meta_tokens.astype(F32), (batch, 1)),
        jnp.zeros((ROW_ALIGN - n_used, d_model), F32)], axis=0)

    clf = cache_logf.reshape(depth, n_phys, SUBLANE, LANE)
    w_in_t = jnp.swapaxes(w_in, 1, 2)
    pad_rows = jnp.zeros((ROW_ALIGN - n_used, attn_w), F32)

    def row(v):
        return v.reshape(1, -1)

    outs = {k: [] for k in ("lfp", "pp", "ks", "vs", "lfs", "ps")}
    z4s = []
    for l in range(depth):
        w_f_t = jnp.pad(w_in_t[l, 3 * attn_w + pool_w:, :], ((0, LANE - n_heads), (0, 0)))
        b_f = jnp.pad(b_forget[l].astype(F32), (0, LANE - n_heads)).reshape(1, LANE)

        h = _ffn_half(h, row(g_pre_ffn1[l]), row(g_post_ffn1[l]), ffn1_gate, ffn1_up, ffn1_down, l)

        z4, lf = _in_proj(h, row(g_pre_mix[l]), w_in_t, w_f_t, b_f, n_heads, l)

        lf_real_t = lf[:n_real].reshape(batch, seq, n_heads).transpose(0, 2, 1).reshape(batch * n_heads, seq)
        lf_meta_t = lf[m0:n_used].reshape(batch, N_META, n_heads).transpose(0, 2, 1).reshape(batch * n_heads, N_META)
        lf_meta_t = jnp.pad(lf_meta_t, ((0, 0), (0, LANE - N_META)))
        f_meta, f_real = _prompt_fcum(lf_meta_t, lf_real_t)
        q_s = z4[0, s0:m0].reshape(n_seq, n_heads, head_dim)
        k_s = z4[1, s0:m0].reshape(n_seq, n_heads, head_dim)
        v_s = z4[2, s0:m0].reshape(n_seq, n_heads, head_dim)
        p_s = z4[3, s0:m0]
        lf_s = lf[s0:m0]
        a_s, a_real, a_meta = _mixed_attention(
            page_table, q_s, k_s, v_s, lf_s.reshape(n_seq, n_heads, 1), cache_k, cache_v, clf, l,
            z4, f_real.reshape(batch * n_heads, 1, seq), f_meta.reshape(batch * n_heads, 1, LANE),
            batch, seq, m0)
        a_s = a_s.reshape(n_seq, attn_w)

        wp = w_pool[l].astype(BF16)
        sc = row(pool_scale[l].astype(F32))
        pm_real = _pool_real(z4, wp, sc, batch, seq, m0)
        pm_meta, pm_s = _pool_small(z4[3, m0:n_used], p_s, state_pool[l].transpose(1, 0, 2), wp, sc)

        a_tail = jnp.concatenate([a_s, a_meta, pad_rows], axis=0)
        pm_tail = jnp.concatenate([pm_s, pm_meta, pad_rows], axis=0)
        h = _out_proj(h, a_real, pm_real, a_tail, pm_tail, w_out, row(g_post_mix[l]), l)

        h = _ffn_half(h, row(g_pre_ffn2[l]), row(g_post_ffn2[l]), ffn2_gate, ffn2_up, ffn2_down, l)

        def seq_major(x_real, x_meta, tail):
            return jnp.concatenate(
                [x_meta.reshape((batch, N_META) + tail), x_real.reshape((batch, seq) + tail)], axis=1)

        z4s.append(z4)
        outs["lfp"].append(seq_major(lf[:n_real], lf[m0:n_used], (n_heads,)))
        outs["pp"].append(jnp.stack([z4[3, (b + 1) * seq - POOL_CTX:(b + 1) * seq] for b in range(batch)], 0))
        outs["ks"].append(k_s.reshape(n_seq, 1, n_heads, head_dim))
        outs["vs"].append(v_s.reshape(n_seq, 1, n_heads, head_dim))
        outs["lfs"].append(lf_s.reshape(n_seq, 1, n_heads))
        outs["ps"].append(jnp.concatenate([state_pool[l][:, 1:], p_s[:, None, :]], axis=1))

    y_prompt = h[:n_real].reshape(batch, seq, d_model)
    y_sample = h[s0:m0].reshape(n_seq, 1, d_model)
    st = {k: jnp.stack(v, 0) for k, v in outs.items()}
    k_prompt, v_prompt = _kv_assemble(z4s, batch, seq, n_heads, m0)
    return (y_prompt, y_sample, k_prompt, v_prompt, st["lfp"], st["pp"],
            st["ks"], st["vs"], st["lfs"], st["ps"])
```

```python
import functools

import jax
import jax.numpy as jnp
from jax import lax
from jax.experimental import pallas as pl
from jax.experimental.pallas import tpu as pltpu

F32 = jnp.float32
BF16 = jnp.bfloat16

N_META = 16
HEAD_DIM = 128
PAGE_SIZE = 128
POOL_WINDOWS = (2, 4, 8, 16)
POOL_CTX = max(POOL_WINDOWS) - 1
RMS_EPS = 1e-6
NEG_INF = -1e30
ATTN_SCALE = HEAD_DIM ** -0.5

LANE = 128
SUBLANE = 8
ROW_TILE = 768
INPROJ_TILE = 1056
ROW_ALIGN = 8448
FF_TILE = 512
POOL_TILE = 512
Q_TILE = 256
KV_PARTS = 2
VMEM_LIMIT = 56 * 1024 * 1024
FFN_VMEM_LIMIT = 60 * 1024 * 1024


def _rms(x, g):
    return x * lax.rsqrt(jnp.mean(x * x, axis=-1, keepdims=True) + RMS_EPS) * g


def _split3(x):
    hi = x.astype(BF16)
    r1 = x - hi.astype(F32)
    mid = r1.astype(BF16)
    lo = (r1 - mid.astype(F32)).astype(BF16)
    return hi, mid, lo


def _dot3(x, w01):
    hi, mid, lo = _split3(x)
    acc = jnp.dot(hi, w01, preferred_element_type=F32)
    acc = acc + jnp.dot(mid, w01, preferred_element_type=F32)
    return acc + jnp.dot(lo, w01, preferred_element_type=F32)


def _dot3_rhs(w01, x):
    hi, mid, lo = _split3(x)
    acc = jnp.dot(w01, hi, preferred_element_type=F32)
    acc = acc + jnp.dot(w01, mid, preferred_element_type=F32)
    return acc + jnp.dot(w01, lo, preferred_element_type=F32)


def _dot_nt(a, b):
    return lax.dot_general(a, b, (((1,), (1,)), ((), ())), preferred_element_type=F32)


def _dot_mixed(a, b):
    return lax.dot_general(a, b, (((1,), (0,)), ((), ())), preferred_element_type=F32)


def _ffn_kernel(x_ref, gpre_ref, gpost_ref, wg_ref, wu_ref, wd_ref, o_ref, xn_ref, *, last_valid):
    j = pl.program_id(1)
    last = pl.num_programs(1) - 1

    def swiglu_part(cols):
        xn = xn_ref[...]
        a = _dot_mixed(xn, wg_ref[:, :cols])
        b = _dot_mixed(xn, wu_ref[:, :cols])
        return _dot_mixed((a * jax.nn.sigmoid(a) * b).astype(BF16), wd_ref[:cols, :])

    @pl.when(j == 0)
    def _():
        xn_ref[...] = _rms(x_ref[...], gpre_ref[...]).astype(BF16)
        o_ref[...] = swiglu_part(wg_ref.shape[1])

    @pl.when((j > 0) & (j < last))
    def _():
        o_ref[...] += swiglu_part(wg_ref.shape[1])

    @pl.when(j == last)
    def _():
        y = o_ref[...] + swiglu_part(last_valid)
        o_ref[...] = x_ref[...] + 0.5 * _rms(y, gpost_ref[...])


def _ffn_half(h, g_pre, g_post, wg, wu, wd, layer):
    rows, d = h.shape
    ff = wg.shape[2]
    n_blocks = pl.cdiv(ff, FF_TILE)
    last_valid = ff - (n_blocks - 1) * FF_TILE
    assert last_valid % LANE == 0
    return pl.pallas_call(
        functools.partial(_ffn_kernel, last_valid=last_valid),
        grid=(rows // ROW_TILE, n_blocks),
        in_specs=[
            pl.BlockSpec((ROW_TILE, d), lambda i, j: (i, 0)),
            pl.BlockSpec((1, d), lambda i, j: (0, 0)),
            pl.BlockSpec((1, d), lambda i, j: (0, 0)),
            pl.BlockSpec((None, d, FF_TILE), lambda i, j: (layer, 0, j)),
            pl.BlockSpec((None, d, FF_TILE), lambda i, j: (layer, 0, j)),
            pl.BlockSpec((None, FF_TILE, d), lambda i, j: (layer, j, 0)),
        ],
        out_specs=pl.BlockSpec((ROW_TILE, d), lambda i, j: (i, 0)),
        out_shape=jax.ShapeDtypeStruct((rows, d), F32),
        scratch_shapes=[pltpu.VMEM((ROW_TILE, d), BF16)],
        compiler_params=pltpu.CompilerParams(
            dimension_semantics=("parallel", "arbitrary"), vmem_limit_bytes=FFN_VMEM_LIMIT),
        name="ffn_half",
    )(h, g_pre, g_post, wg, wu, wd)


def _inproj_kernel(h_ref, g_ref, w_ref, wf_ref, bf_ref, z_ref, lf_ref, u_ref):
    n = pl.program_id(1)

    @pl.when(n == 0)
    def _():
        u = _rms(h_ref[...], g_ref[...]).astype(BF16)
        u_ref[...] = u
        x = _dot_nt(u, wf_ref[...]) + bf_ref[...]
        lf = jnp.minimum(x, 0.0) - jnp.log1p(jnp.exp(-jnp.abs(x)))
        lf_ref[...] = lf[:, :lf_ref.shape[1]]

    z_ref[0] = _dot_nt(u_ref[...], w_ref[...])


def _in_proj(h, g, w_in_t, w_f_t, b_f, n_heads, layer):
    rows, d = h.shape
    col_tile = (w_in_t.shape[1] - n_heads) // 4
    return pl.pallas_call(
        _inproj_kernel,
        grid=(rows // INPROJ_TILE, 4),
        in_specs=[
            pl.BlockSpec((INPROJ_TILE, d), lambda i, n: (i, 0)),
            pl.BlockSpec((1, d), lambda i, n: (0, 0)),
            pl.BlockSpec((None, col_tile, d), lambda i, n: (layer, n, 0)),
            pl.BlockSpec((LANE, d), lambda i, n: (0, 0)),
            pl.BlockSpec((1, LANE), lambda i, n: (0, 0)),
        ],
        out_specs=[
            pl.BlockSpec((1, INPROJ_TILE, col_tile), lambda i, n: (n, i, 0)),
            pl.BlockSpec((INPROJ_TILE, n_heads), lambda i, n: (i, 0)),
        ],
        out_shape=[
            jax.ShapeDtypeStruct((4, rows, col_tile), F32),
            jax.ShapeDtypeStruct((rows, n_heads), F32),
        ],
        scratch_shapes=[pltpu.VMEM((INPROJ_TILE, d), BF16)],
        compiler_params=pltpu.CompilerParams(
            dimension_semantics=("parallel", "arbitrary"), vmem_limit_bytes=VMEM_LIMIT),
        name="in_proj",
    )(h, g, w_in_t, w_f_t, b_f)


def _ones_where(shape, pred):
    r = lax.broadcasted_iota(jnp.int32, shape, 0)
    c = lax.broadcasted_iota(jnp.int32, shape, 1)
    return jnp.where(pred(r, c), 1.0, 0.0).astype(BF16)


def _fcum_kernel(lfm_ref, lfr_ref, fm_ref, fr_ref):
    u = _ones_where((LANE, LANE), lambda r, c: r <= c)
    fm = _dot3(lfm_ref[...], u)
    fm_ref[...] = fm
    carry = fm[:, LANE - 1:LANE]
    for blk in range(lfr_ref.shape[1] // LANE):
        sl = slice(blk * LANE, (blk + 1) * LANE)
        f = _dot3(lfr_ref[:, sl], u) + carry
        fr_ref[:, sl] = f
        carry = f[:, LANE - 1:LANE]


def _prompt_fcum(lf_meta_t, lf_real_t):
    return pl.pallas_call(
        _fcum_kernel,
        out_shape=[jax.ShapeDtypeStruct(lf_meta_t.shape, F32), jax.ShapeDtypeStruct(lf_real_t.shape, F32)],
        name="prompt_fcum",
    )(lf_meta_t, lf_real_t)


def _softmax_pv(parts):
    m = None
    for s, _ in parts:
        mi = jnp.max(s, axis=-1, keepdims=True)
        m = mi if m is None else jnp.maximum(m, mi)
    l = None
    o = None
    for s, v in parts:
        p = jnp.exp(s - m)
        li = jnp.sum(p, axis=-1, keepdims=True)
        oi = jnp.dot(p.astype(BF16), v, preferred_element_type=F32)
        l = li if l is None else l + li
        o = oi if o is None else o + oi
    return o / l


def _causal(s, q0):
    r = lax.broadcasted_iota(jnp.int32, s.shape, 0) + q0
    c = lax.broadcasted_iota(jnp.int32, s.shape, 1)
    return jnp.where(c <= r, s, NEG_INF)


def _prompt_attn_tiles(q_ref, k_ref, v_ref, qm_ref, km_ref, vm_ref, fr_ref, fm_ref, o_ref, om_ref,
                        kb_ref, vb_ref, tiles, first):
    kmb = km_ref[0].astype(BF16)
    vmb = vm_ref[0].astype(BF16)
    f_meta = fm_ref[0][:, :N_META]

    if first:
        kb_ref[...] = k_ref[0].astype(BF16)
        vb_ref[...] = v_ref[0].astype(BF16)
        qmb = (qm_ref[0] * ATTN_SCALE).astype(BF16)
        sm = _causal(_dot_nt(qmb, kmb) - f_meta, 0)
        om_ref[...] = _softmax_pv([(sm, vmb)])

    for qi in tiles:
        q0 = qi * Q_TILE
        n = q0 + Q_TILE
        qb = (q_ref[0, q0:n, :] * ATTN_SCALE).astype(BF16)
        s_diag = _causal(_dot_nt(qb, kb_ref[q0:n, :]) - fr_ref[0, :, q0:n], 0)
        parts = [(s_diag, vb_ref[q0:n, :]), (_dot_nt(qb, kmb) - f_meta, vmb)]
        if q0 > 0:
            parts.append((_dot_nt(qb, kb_ref[:q0, :]) - fr_ref[0, :, :q0], vb_ref[:q0, :]))
        o_ref[q0:n, :] = _softmax_pv(parts)


def _decode_step(pt_ref, q_ref, kn_ref, vn_ref, lfn_ref, ck_ref, cv_ref, clf_ref, o_ref,
                 kbuf, vbuf, lfbuf, f_ref, sems, *, layer, n_pages):
    b = pl.program_id(0)
    nb = pl.num_programs(0)
    half = n_pages // 2
    n_heads = q_ref.shape[1]
    keys_half = half * PAGE_SIZE * n_heads
    n_chunks = keys_half // LANE
    slot = b % 2

    def copies(seq, slot_):
        out = []
        for j in range(n_pages):
            page = pt_ref[seq * n_pages + j]
            dst_lanes = pl.ds((j // half) * HEAD_DIM, HEAD_DIM)
            out.append(pltpu.make_async_copy(
                ck_ref.at[layer, page], kbuf.at[slot_, j % half, :, :, dst_lanes], sems.at[0, slot_]))
            out.append(pltpu.make_async_copy(
                cv_ref.at[layer, page], vbuf.at[slot_, j % half, :, :, dst_lanes], sems.at[1, slot_]))
            out.append(pltpu.make_async_copy(
                clf_ref.at[layer, page], lfbuf.at[slot_, j], sems.at[2, slot_]))
        return out

    @pl.when(b == 0)
    def _():
        for c in copies(0, 0):
            c.start()

    @pl.when(b + 1 < nb)
    def _():
        for c in copies(b + 1, 1 - slot):
            c.start()

    for c in copies(b, slot):
        c.wait()

    x = lfbuf[slot].reshape(n_pages * SUBLANE, LANE)
    nr = x.shape[0]
    w_in_row = _ones_where((LANE, LANE), lambda r, c: (r % n_heads == c % n_heads) & (r <= c))
    w_row_tot = _ones_where((LANE, LANE), lambda r, c: r % n_heads == c % n_heads)
    l_strict = _ones_where((nr, nr), lambda r, c: c < r)
    f_all = _dot3(x, w_in_row) + _dot3_rhs(l_strict, _dot3(x, w_row_tot))
    f_ref[...] = f_all

    q = q_ref[0] * ATTN_SCALE
    zero = jnp.zeros_like(q)
    q2 = jnp.concatenate([jnp.concatenate([q, zero], axis=1),
                          jnp.concatenate([zero, q], axis=1)], axis=0)
    kmat = kbuf[slot].reshape(keys_half, 2 * HEAD_DIM)
    s2 = _dot_nt(q2, kmat)

    f_lo = jnp.concatenate(
        [jnp.broadcast_to(f_ref[c:c + 1, :], (n_heads, LANE)) for c in range(n_chunks)], axis=1)
    f_hi = jnp.concatenate(
        [jnp.broadcast_to(f_ref[n_chunks + c:n_chunks + c + 1, :], (n_heads, LANE)) for c in range(n_chunks)], axis=1)
    f2 = jnp.concatenate([f_lo, f_hi], axis=0)

    row = lax.broadcasted_iota(jnp.int32, (n_heads, LANE), 0)
    col = lax.broadcasted_iota(jnp.int32, (n_heads, LANE), 1)
    bias = jnp.where(col % n_heads == row, 0.0, -jnp.inf)
    bias = jnp.concatenate([bias] * n_chunks, axis=1)
    bias = jnp.concatenate([bias, bias], axis=0)
    logit = s2 - f2 + bias
    m2 = jnp.max(logit, axis=-1, keepdims=True)
    m = jnp.maximum(m2[:n_heads], m2[n_heads:])
    p = jnp.exp(logit - jnp.concatenate([m, m], axis=0))
    l2 = jnp.sum(p, axis=-1, keepdims=True)
    l = l2[:n_heads] + l2[n_heads:]
    vmat = vbuf[slot].reshape(keys_half, 2 * HEAD_DIM)
    o2 = jnp.dot(p, vmat, preferred_element_type=F32)
    o = o2[:n_heads, :HEAD_DIM] + o2[n_heads:, HEAD_DIM:]

    last = jnp.broadcast_to(f_ref[nr - 1:nr, :], (n_heads, LANE))
    r8 = lax.broadcasted_iota(jnp.int32, (n_heads, LANE), 0)
    c8 = lax.broadcasted_iota(jnp.int32, (n_heads, LANE), 1)
    f_past = jnp.sum(jnp.where(c8 == LANE - n_heads + r8, last, 0.0), axis=-1, keepdims=True)
    s_new = jnp.sum(q * kn_ref[0], axis=-1, keepdims=True)
    logit_new = s_new - (f_past + lfn_ref[0])
    m_fin = jnp.maximum(m, logit_new)
    alpha = jnp.exp(m - m_fin)
    p_new = jnp.exp(logit_new - m_fin)
    o_ref[0] = (alpha * o + p_new * vn_ref[0]) / (alpha * l + p_new)


def _mixed_attn_kernel(pt_ref, q_ref, kn_ref, vn_ref, lfn_ref, ck_ref, cv_ref, clf_ref,
                       pq_ref, pk_ref, pv_ref, pqm_ref, pkm_ref, pvm_ref, fr_ref, fm_ref,
                       o_ref, po_ref, pom_ref, kbuf, vbuf, lfbuf, f_ref, sems, pkb_ref, pvb_ref,
                       *, layer, n_pages, per):
    _decode_step(pt_ref, q_ref, kn_ref, vn_ref, lfn_ref, ck_ref, cv_ref, clf_ref, o_ref,
                 kbuf, vbuf, lfbuf, f_ref, sems, layer=layer, n_pages=n_pages)
    n_tiles = pq_ref.shape[1] // Q_TILE
    part = pl.program_id(0) % per
    for s_ in range(per):
        tiles = (s_, n_tiles - 1 - s_)

        @pl.when(part == s_)
        def _():
            _prompt_attn_tiles(pq_ref, pk_ref, pv_ref, pqm_ref, pkm_ref, pvm_ref, fr_ref, fm_ref,
                               po_ref, pom_ref, pkb_ref, pvb_ref, tiles, first=(s_ == 0))


def _mixed_attention(page_table, q, k_new, v_new, lf_new, cache_k, cache_v, cache_lf, layer,
                     z4, f_real, f_meta, batch, seq, meta_row0):
    n_seq, n_pages = page_table.shape
    _, n_heads, head_dim = q.shape
    half = n_pages // 2
    units = batch * n_heads
    per = n_seq // units
    assert n_pages % 2 == 0 and PAGE_SIZE * n_heads == SUBLANE * LANE and head_dim == HEAD_DIM
    assert n_seq == per * units and seq == 2 * per * Q_TILE
    meta_blk0 = meta_row0 // N_META

    def row_map(b, pt):
        return (b, 0, 0)

    def zspec(which, meta):
        if meta:
            return pl.BlockSpec((1, N_META, HEAD_DIM),
                                lambda b, pt: (which, meta_blk0 + (b // per) // n_heads, (b // per) % n_heads))
        return pl.BlockSpec((1, seq, HEAD_DIM), lambda b, pt: (which, (b // per) // n_heads, (b // per) % n_heads))

    def unit_map(b, pt):
        return ((b // per) // n_heads, (b // per) % n_heads)

    any_spec = pl.BlockSpec(memory_space=pl.ANY)
    grid_spec = pltpu.PrefetchScalarGridSpec(
        num_scalar_prefetch=1,
        grid=(n_seq,),
        in_specs=[
            pl.BlockSpec((1, n_heads, head_dim), row_map),
            pl.BlockSpec((1, n_heads, head_dim), row_map),
            pl.BlockSpec((1, n_heads, head_dim), row_map),
            pl.BlockSpec((1, n_heads, 1), row_map),
            any_spec, any_spec, any_spec,
            zspec(0, False), zspec(1, False), zspec(2, False),
            zspec(0, True), zspec(1, True), zspec(2, True),
            pl.BlockSpec((1, 1, seq), lambda b, pt: (b // per, 0, 0)),
            pl.BlockSpec((1, 1, LANE), lambda b, pt: (b // per, 0, 0)),
        ],
        out_specs=[
            pl.BlockSpec((1, n_heads, head_dim), row_map),
            pl.BlockSpec((seq, HEAD_DIM), unit_map),
            pl.BlockSpec((N_META, HEAD_DIM), unit_map),
        ],
        scratch_shapes=[
            pltpu.VMEM((2, half, PAGE_SIZE, n_heads, 2 * head_dim), F32),
            pltpu.VMEM((2, half, PAGE_SIZE, n_heads, 2 * head_dim), F32),
            pltpu.VMEM((2, n_pages, SUBLANE, LANE), F32),
            pltpu.VMEM((n_pages * SUBLANE, LANE), F32),
            pltpu.SemaphoreType.DMA((3, 2)),
            pltpu.VMEM((seq, HEAD_DIM), BF16),
            pltpu.VMEM((seq, HEAD_DIM), BF16),
        ],
    )
    return pl.pallas_call(
        functools.partial(_mixed_attn_kernel, layer=layer, n_pages=n_pages, per=per),
        grid_spec=grid_spec,
        out_shape=[
            jax.ShapeDtypeStruct((n_seq, n_heads, head_dim), F32),
            jax.ShapeDtypeStruct((batch * seq, n_heads * HEAD_DIM), F32),
            jax.ShapeDtypeStruct((batch * N_META, n_heads * HEAD_DIM), F32),
        ],
        compiler_params=pltpu.CompilerParams(
            dimension_semantics=("arbitrary",), vmem_limit_bytes=VMEM_LIMIT),
        name="mixed_attn",
    )(page_table.reshape(-1), q, k_new, v_new, lf_new, cache_k, cache_v, cache_lf,
      z4, z4, z4, z4, z4, z4, f_real, f_meta)


def _pool_project(d_groups, wp_ref, sc_ref, o_ref):
    for g, d in enumerate(d_groups):
        gw = d.shape[1]
        cs = slice(g * gw, (g + 1) * gw)
        o_ref[:, cs] = jnp.dot(d.astype(BF16), wp_ref[g], preferred_element_type=F32) * sc_ref[:, cs]


def _pool_real_kernel(halo_ref, cur_ref, wp_ref, sc_ref, o_ref, ext_ref):
    rows = cur_ref.shape[0]
    gw = cur_ref.shape[1] // len(POOL_WINDOWS)
    ext_ref[0:N_META, :] = halo_ref[...]
    ext_ref[N_META:, :] = cur_ref[...]
    ds = []
    for g, w in enumerate(POOL_WINDOWS):
        cs = slice(g * gw, (g + 1) * gw)
        cur = ext_ref[N_META:N_META + rows, cs]
        s = cur
        for back in range(1, w):
            s = s + ext_ref[N_META - back:N_META - back + rows, cs]
        ds.append(s * (1.0 / w) - cur)
    _pool_project(ds, wp_ref, sc_ref, o_ref)


def _pool_real(z4, w_pool, scale, batch, seq, meta_row0):
    width = z4.shape[2]
    tiles = seq // POOL_TILE
    per16 = POOL_TILE // N_META
    meta_blk0 = meta_row0 // N_META

    def halo_map(b, i):
        return (3, jnp.where(i == 0, meta_blk0 + b, (b * tiles + i) * per16 - 1), 0)

    return pl.pallas_call(
        _pool_real_kernel,
        grid=(batch, tiles),
        in_specs=[
            pl.BlockSpec((None, N_META, width), halo_map),
            pl.BlockSpec((None, POOL_TILE, width), lambda b, i: (3, b * tiles + i, 0)),
            pl.BlockSpec(w_pool.shape, lambda b, i: (0, 0, 0)),
            pl.BlockSpec((1, width), lambda b, i: (0, 0)),
        ],
        out_specs=pl.BlockSpec((POOL_TILE, width), lambda b, i: (b * tiles + i, 0)),
        out_shape=jax.ShapeDtypeStruct((batch * seq, width), F32),
        scratch_shapes=[pltpu.VMEM((N_META + POOL_TILE, width), F32)],
        compiler_params=pltpu.CompilerParams(
            dimension_semantics=("parallel", "parallel"), vmem_limit_bytes=VMEM_LIMIT),
        name="pool_real",
    )(z4, z4, w_pool, scale)


def _pool_small_kernel(pm_ref, ps_ref, st_ref, wp_ref, sc_ref, om_ref, os_ref, ext_ref):
    width = pm_ref.shape[1]
    gw = width // len(POOL_WINDOWS)
    ext_ref[0:N_META, :] = jnp.zeros((N_META, width), F32)
    pos = lax.broadcasted_iota(jnp.int32, (N_META, 1), 0)
    for b in range(pm_ref.shape[0] // N_META):
        ext_ref[N_META:, :] = pm_ref[b * N_META:(b + 1) * N_META, :]
        ds = []
        for g, w in enumerate(POOL_WINDOWS):
            cs = slice(g * gw, (g + 1) * gw)
            cur = ext_ref[N_META:2 * N_META, cs]
            s = cur
            for back in range(1, w):
                s = s + ext_ref[N_META - back:2 * N_META - back, cs]
            cnt = jnp.minimum(pos + 1, w).astype(F32)
            ds.append(s / cnt - cur)
        _pool_project(ds, wp_ref, sc_ref, om_ref.at[b * N_META:(b + 1) * N_META, :])
    ds = []
    for g, w in enumerate(POOL_WINDOWS):
        cs = slice(g * gw, (g + 1) * gw)
        cur = ps_ref[:, cs]
        s = cur
        for back in range(1, w):
            s = s + st_ref[POOL_CTX - back, :, cs]
        ds.append(s * (1.0 / w) - cur)
    _pool_project(ds, wp_ref, sc_ref, os_ref)


def _pool_small(p_meta, p_sample, state_t, w_pool, scale):
    return pl.pallas_call(
        _pool_small_kernel,
        out_shape=[jax.ShapeDtypeStruct(p_meta.shape, F32), jax.ShapeDtypeStruct(p_sample.shape, F32)],
        scratch_shapes=[pltpu.VMEM((2 * N_META, p_meta.shape[1]), F32)],
        compiler_params=pltpu.CompilerParams(vmem_limit_bytes=VMEM_LIMIT),
        name="pool_small",
    )(p_meta, p_sample, state_t, w_pool, scale)


def _outproj_kernel(h_ref, a_ref, pm_ref, at_ref, pmt_ref, w_ref, g_ref, o_ref):
    i = pl.program_id(0)
    n_real_tiles = pl.num_programs(0) - 1

    def project(a_src, pm_src):
        aw = a_src.shape[1]
        mix = _dot_mixed(a_src[...].astype(BF16), w_ref[0:aw, :])
        mix = mix + _dot_mixed(pm_src[...].astype(BF16), w_ref[aw:, :])
        o_ref[...] = h_ref[...] + _rms(mix, g_ref[...])

    @pl.when(i < n_real_tiles)
    def _():
        project(a_ref, pm_ref)

    @pl.when(i == n_real_tiles)
    def _():
        project(at_ref, pmt_ref)


def _out_proj(h, a_real, pm_real, a_tail, pm_tail, w_out, g, layer):
    rows, d = h.shape
    n_real = a_real.shape[0]
    tile = rows - n_real
    assert a_tail.shape[0] == tile and n_real % tile == 0
    n_real_tiles = n_real // tile

    def real_map(i):
        return (jnp.minimum(i, n_real_tiles - 1), 0)

    return pl.pallas_call(
        _outproj_kernel,
        grid=(n_real_tiles + 1,),
        in_specs=[
            pl.BlockSpec((tile, d), lambda i: (i, 0)),
            pl.BlockSpec((tile, a_real.shape[1]), real_map),
            pl.BlockSpec((tile, pm_real.shape[1]), real_map),
            pl.BlockSpec((tile, a_tail.shape[1]), lambda i: (0, 0)),
            pl.BlockSpec((tile, pm_tail.shape[1]), lambda i: (0, 0)),
            pl.BlockSpec((None,) + w_out.shape[1:], lambda i: (layer, 0, 0), pipeline_mode=pl.Buffered(1)),
            pl.BlockSpec((1, d), lambda i: (0, 0)),
        ],
        out_specs=pl.BlockSpec((tile, d), lambda i: (i, 0)),
        out_shape=jax.ShapeDtypeStruct((rows, d), F32),
        compiler_params=pltpu.CompilerParams(
            dimension_semantics=("arbitrary",), vmem_limit_bytes=VMEM_LIMIT),
        name="out_proj",
    )(h, a_real, pm_real, a_tail, pm_tail, w_out, g)


def _kv_assemble_kernel(*refs, depth, n_heads):
    z_refs = refs[:2 * depth]
    k_ref, v_ref = refs[2 * depth:]
    layer = pl.program_id(0)
    tensor = pl.program_id(2)
    part = pl.program_id(3)
    rows = z_refs[0].shape[0]

    def retile(real_ref, meta_ref, o_ref, p):
        for h in range(n_heads):
            lanes = slice(h * HEAD_DIM, (h + 1) * HEAD_DIM)
            if p == 0:
                o_ref[0:N_META, h, :] = meta_ref[:, lanes]
            o_ref[N_META + p * rows:N_META + (p + 1) * rows, h, :] = real_ref[:, lanes]

    for l in range(depth):
        for t, o_ref in enumerate((k_ref, v_ref)):
            for p in range(KV_PARTS):
                @pl.when((layer == l) & (tensor == t) & (part == p))
                def _():
                    retile(z_refs[2 * l], z_refs[2 * l + 1], o_ref, p)


def _kv_assemble(z4s, batch, seq, n_heads, meta_row0):
    depth = len(z4s)
    width = z4s[0].shape[2]
    meta_blk0 = meta_row0 // N_META
    rows = seq // KV_PARTS
    shape = jax.ShapeDtypeStruct((depth, batch, N_META + seq, n_heads, HEAD_DIM), F32)

    in_specs = []
    args = []
    for l, z in enumerate(z4s):
        def pick(l_, b, t, p, l=l):
            on = l_ == l
            early = l_ < l
            return (jnp.where(on, b, jnp.where(early, 0, batch - 1)),
                    jnp.where(on, t, jnp.where(early, 0, 1)),
                    jnp.where(on, p, jnp.where(early, 0, KV_PARTS - 1)))

        def real_map(l_, b, t, p, pick=pick):
            bb, tt, pp = pick(l_, b, t, p)
            return (1 + tt, bb * KV_PARTS + pp, 0)

        def meta_map(l_, b, t, p, pick=pick):
            bb, tt, _ = pick(l_, b, t, p)
            return (1 + tt, meta_blk0 + bb, 0)

        in_specs.append(pl.BlockSpec((None, rows, width), real_map))
        in_specs.append(pl.BlockSpec((None, N_META, width), meta_map))
        args += [z, z]

    out_spec = pl.BlockSpec((None, None, N_META + seq, n_heads, HEAD_DIM),
                            lambda l_, b, t, p: (l_, b, 0, 0, 0))
    return pl.pallas_call(
        functools.partial(_kv_assemble_kernel, depth=depth, n_heads=n_heads),
        grid=(depth, batch, 2, KV_PARTS),
        in_specs=in_specs,
        out_specs=[out_spec, out_spec],
        out_shape=[shape, shape],
        compiler_params=pltpu.CompilerParams(
            dimension_semantics=("arbitrary",) * 4, vmem_limit_bytes=VMEM_LIMIT),
        name="kv_assemble",
    )(*args)


def kernel(x_prompt, x_sample, cache_k, cache_v, cache_logf, state_pool, page_table, meta_tokens, w_in, b_forget, w_pool, pool_scale, w_out, ffn1_gate, ffn1_up, ffn1_down, ffn2_gate, ffn2_up, ffn2_down, g_pre_ffn1, g_post_ffn1, g_pre_mix, g_post_mix, g_pre_ffn2, g_post_ffn2):
    batch, seq, d_model = x_prompt.shape
    n_seq = x_sample.shape[0]
    depth, n_phys, page, n_heads, head_dim = cache_k.shape
    attn_w = n_heads * head_dim
    pool_w = state_pool.shape[-1]
    assert head_dim == HEAD_DIM and page == PAGE_SIZE and x_sample.shape[1] == 1
    assert attn_w == pool_w and w_in.shape[-1] == 3 * attn_w + pool_w + n_heads

    n_real = batch * seq
    s0 = n_real
    m0 = s0 + n_seq
    n_used = m0 + batch * N_META
    assert n_used <= ROW_ALIGN and m0 % N_META == 0 and seq % POOL_TILE == 0 and seq % Q_TILE == 0

    h = jnp.concatenate([
        x_prompt.reshape(n_real, d_model),
        x_sample.reshape(n_seq, d_model),
        jnp.tile(meta_tokens.astype(F32), (batch, 1)),
        jnp.zeros((ROW_ALIGN - n_used, d_model), F32)], axis=0)

    clf = cache_logf.reshape(depth, n_phys, SUBLANE, LANE)
    w_in_t = jnp.swapaxes(w_in, 1, 2)
    pad_rows = jnp.zeros((ROW_ALIGN - n_used, attn_w), F32)

    def row(v):
        return v.reshape(1, -1)

    outs = {k: [] for k in ("lfp", "pp", "ks", "vs", "lfs", "ps")}
    z4s = []
    for l in range(depth):
        w_f_t = jnp.pad(w_in_t[l, 3 * attn_w + pool_w:, :], ((0, LANE - n_heads), (0, 0)))
        b_f = jnp.pad(b_forget[l].astype(F32), (0, LANE - n_heads)).reshape(1, LANE)

        h = _ffn_half(h, row(g_pre_ffn1[l]), row(g_post_ffn1[l]), ffn1_gate, ffn1_up, ffn1_down, l)

        z4, lf = _in_proj(h, row(g_pre_mix[l]), w_in_t, w_f_t, b_f, n_heads, l)

        lf_real_t = lf[:n_real].reshape(batch, seq, n_heads).transpose(0, 2, 1).reshape(batch * n_heads, seq)
        lf_meta_t = lf[m0:n_used].reshape(batch, N_META, n_heads).transpose(0, 2, 1).reshape(batch * n_heads, N_META)
        lf_meta_t = jnp.pad(lf_meta_t, ((0, 0), (0, LANE - N_META)))
        f_meta, f_real = _prompt_fcum(lf_meta_t, lf_real_t)
        q_s = z4[0, s0:m0].reshape(n_seq, n_heads, head_dim)
        k_s = z4[1, s0:m0].reshape(n_seq, n_heads, head_dim)
        v_s = z4[2, s0:m0].reshape(n_seq, n_heads, head_dim)
        p_s = z4[3, s0:m0]
        lf_s = lf[s0:m0]
        a_s, a_real, a_meta = _mixed_attention(
            page_table, q_s, k_s, v_s, lf_s.reshape(n_seq, n_heads, 1), cache_k, cache_v, clf, l,
            z4, f_real.reshape(batch * n_heads, 1, seq), f_meta.reshape(batch * n_heads, 1, LANE),
            batch, seq, m0)
        a_s = a_s.reshape(n_seq, attn_w)

        wp = w_pool[l].astype(BF16)
        sc = row(pool_scale[l].astype(F32))
        pm_real = _pool_real(z4, wp, sc, batch, seq, m0)
        pm_meta, pm_s = _pool_small(z4[3, m0:n_used], p_s, state_pool[l].transpose(1, 0, 2), wp, sc)

        a_tail = jnp.concatenate([a_s, a_meta, pad_rows], axis=0)
        pm_tail = jnp.concatenate([pm_s, pm_meta, pad_rows], axis=0)
        h = _out_proj(h, a_real, pm_real, a_tail, pm_tail, w_out, row(g_post_mix[l]), l)

        h = _ffn_half(h, row(g_pre_ffn2[l]), row(g_post_ffn2[l]), ffn2_gate, ffn2_up, ffn2_down, l)

        def seq_major(x_real, x_meta, tail):
            return jnp.concatenate(
                [x_meta.reshape((batch, N_META) + tail), x_real.reshape((batch, seq) + tail)], axis=1)

        z4s.append(z4)
        outs["lfp"].append(seq_major(lf[:n_real], lf[m0:n_used], (n_heads,)))
        outs["pp"].append(jnp.stack([z4[3, (b + 1) * seq - POOL_CTX:(b + 1) * seq] for b in range(batch)], 0))
        outs["ks"].append(k_s.reshape(n_seq, 1, n_heads, head_dim))
        outs["vs"].append(v_s.reshape(n_seq, 1, n_heads, head_dim))
        outs["lfs"].append(lf_s.reshape(n_seq, 1, n_heads))
        outs["ps"].append(jnp.concatenate([state_pool[l][:, 1:], p_s[:, None, :]], axis=1))

    y_prompt = h[:n_real].reshape(batch, seq, d_model)
    y_sample = h[s0:m0].reshape(n_seq, 1, d_model)
    st = {k: jnp.stack(v, 0) for k, v in outs.items()}
    k_prompt, v_prompt = _kv_assemble(z4s, batch, seq, n_heads, m0)
    return (y_prompt, y_sample, k_prompt, v_prompt, st["lfp"], st["pp"],
            st["ks"], st["vs"], st["lfs"], st["ps"])
```

```python
import functools

import jax
import jax.numpy as jnp
from jax import lax
from jax.experimental import pallas as pl
from jax.experimental.pallas import tpu as pltpu

F32 = jnp.float32
BF16 = jnp.bfloat16

N_META = 16
HEAD_DIM = 128
PAGE_SIZE = 128
POOL_WINDOWS = (2, 4, 8, 16)
POOL_CTX = max(POOL_WINDOWS) - 1
RMS_EPS = 1e-6
NEG_INF = -1e30
ATTN_SCALE = HEAD_DIM ** -0.5

LANE = 128
SUBLANE = 8
ROW_TILE = 768
INPROJ_TILE = 1056
ROW_ALIGN = 8448
FF_TILE = 512
POOL_TILE = 512
Q_TILE = 256
KV_PARTS = 2
VMEM_LIMIT = 56 * 1024 * 1024
FFN_VMEM_LIMIT = 60 * 1024 * 1024


def _rms(x, g):
    return x * lax.rsqrt(jnp.mean(x * x, axis=-1, keepdims=True) + RMS_EPS) * g


def _split3(x):
    hi = x.astype(BF16)
    r1 = x - hi.astype(F32)
    mid = r1.astype(BF16)
    lo = (r1 - mid.astype(F32)).astype(BF16)
    return hi, mid, lo


def _dot3(x, w01):
    hi, mid, lo = _split3(x)
    acc = jnp.dot(hi, w01, preferred_element_type=F32)
    acc = acc + jnp.dot(mid, w01, preferred_element_type=F32)
    return acc + jnp.dot(lo, w01, preferred_element_type=F32)


def _dot3_rhs(w01, x):
    hi, mid, lo = _split3(x)
    acc = jnp.dot(w01, hi, preferred_element_type=F32)
    acc = acc + jnp.dot(w01, mid, preferred_element_type=F32)
    return acc + jnp.dot(w01, lo, preferred_element_type=F32)


def _dot_nt(a, b):
    return lax.dot_general(a, b, (((1,), (1,)), ((), ())), preferred_element_type=F32)


def _dot_mixed(a, b):
    return lax.dot_general(a, b, (((1,), (0,)), ((), ())), preferred_element_type=F32)


def _ffn_kernel(x_ref, gpre_ref, gpost_ref, wg_ref, wu_ref, wd_ref, o_ref, xn_ref, *, last_valid):
    j = pl.program_id(1)
    last = pl.num_programs(1) - 1

    def swiglu_part(cols):
        xn = xn_ref[...]
        a = _dot_mixed(xn, wg_ref[:, :cols])
        b = _dot_mixed(xn, wu_ref[:, :cols])
        return _dot_mixed((a * jax.nn.sigmoid(a) * b).astype(BF16), wd_ref[:cols, :])

    @pl.when(j == 0)
    def _():
        xn_ref[...] = _rms(x_ref[...], gpre_ref[...]).astype(BF16)
        o_ref[...] = swiglu_part(wg_ref.shape[1])

    @pl.when((j > 0) & (j < last))
    def _():
        o_ref[...] += swiglu_part(wg_ref.shape[1])

    @pl.when(j == last)
    def _():
        y = o_ref[...] + swiglu_part(last_valid)
        o_ref[...] = x_ref[...] + 0.5 * _rms(y, gpost_ref[...])


def _ffn_half(h, g_pre, g_post, wg, wu, wd, layer):
    rows, d = h.shape
    ff = wg.shape[2]
    n_blocks = pl.cdiv(ff, FF_TILE)
    last_valid = ff - (n_blocks - 1) * FF_TILE
    assert last_valid % LANE == 0
    return pl.pallas_call(
        functools.partial(_ffn_kernel, last_valid=last_valid),
        grid=(rows // ROW_TILE, n_blocks),
        in_specs=[
            pl.BlockSpec((ROW_TILE, d), lambda i, j: (i, 0)),
            pl.BlockSpec((1, d), lambda i, j: (0, 0)),
            pl.BlockSpec((1, d), lambda i, j: (0, 0)),
            pl.BlockSpec((None, d, FF_TILE), lambda i, j: (layer, 0, j)),
            pl.BlockSpec((None, d, FF_TILE), lambda i, j: (layer, 0, j)),
            pl.BlockSpec((None, FF_TILE, d), lambda i, j: (layer, j, 0)),
        ],
        out_specs=pl.BlockSpec((ROW_TILE, d), lambda i, j: (i, 0)),
        out_shape=jax.ShapeDtypeStruct((rows, d), F32),
        scratch_shapes=[pltpu.VMEM((ROW_TILE, d), BF16)],
        compiler_params=pltpu.CompilerParams(
            dimension_semantics=("parallel", "arbitrary"), vmem_limit_bytes=FFN_VMEM_LIMIT),
        name="ffn_half",
    )(h, g_pre, g_post, wg, wu, wd)


def _inproj_kernel(h_ref, g_ref, w_ref, wf_ref, bf_ref, z_ref, lf_ref, u_ref):
    n = pl.program_id(1)

    @pl.when(n == 0)
    def _():
        u = _rms(h_ref[...], g_ref[...]).astype(BF16)
        u_ref[...] = u
        x = _dot_nt(u, wf_ref[...]) + bf_ref[...]
        lf = jnp.minimum(x, 0.0) - jnp.log1p(jnp.exp(-jnp.abs(x)))
        lf_ref[...] = lf[:, :lf_ref.shape[1]]

    z_ref[0] = _dot_nt(u_ref[...], w_ref[...])


def _in_proj(h, g, w_in_t, w_f_t, b_f, n_heads, layer):
    rows, d = h.shape
    col_tile = (w_in_t.shape[1] - n_heads) // 4
    return pl.pallas_call(
        _inproj_kernel,
        grid=(rows // INPROJ_TILE, 4),
        in_specs=[
            pl.BlockSpec((INPROJ_TILE, d), lambda i, n: (i, 0)),
            pl.BlockSpec((1, d), lambda i, n: (0, 0)),
            pl.BlockSpec((None, col_tile, d), lambda i, n: (layer, n, 0)),
            pl.BlockSpec((LANE, d), lambda i, n: (0, 0)),
            pl.BlockSpec((1, LANE), lambda i, n: (0, 0)),
        ],
        out_specs=[
            pl.BlockSpec((1, INPROJ_TILE, col_tile), lambda i, n: (n, i, 0)),
            pl.BlockSpec((INPROJ_TILE, n_heads), lambda i, n: (i, 0)),
        ],
        out_shape=[
            jax.ShapeDtypeStruct((4, rows, col_tile), F32),
            jax.ShapeDtypeStruct((rows, n_heads), F32),
        ],
        scratch_shapes=[pltpu.VMEM((INPROJ_TILE, d), BF16)],
        compiler_params=pltpu.CompilerParams(
            dimension_semantics=("parallel", "arbitrary"), vmem_limit_bytes=VMEM_LIMIT),
        name="in_proj",
    )(h, g, w_in_t, w_f_t, b_f)


def _ones_where(shape, pred):
    r = lax.broadcasted_iota(jnp.int32, shape, 0)
    c = lax.broadcasted_iota(jnp.int32, shape, 1)
    return jnp.where(pred(r, c), 1.0, 0.0).astype(BF16)


def _fcum_kernel(lfm_ref, lfr_ref, fm_ref, fr_ref):
    u = _ones_where((LANE, LANE), lambda r, c: r <= c)
    fm = _dot3(lfm_ref[...], u)
    fm_ref[...] = fm
    carry = fm[:, LANE - 1:LANE]
    for blk in range(lfr_ref.shape[1] // LANE):
        sl = slice(blk * LANE, (blk + 1) * LANE)
        f = _dot3(lfr_ref[:, sl], u) + carry
        fr_ref[:, sl] = f
        carry = f[:, LANE - 1:LANE]


def _prompt_fcum(lf_meta_t, lf_real_t):
    return pl.pallas_call(
        _fcum_kernel,
        out_shape=[jax.ShapeDtypeStruct(lf_meta_t.shape, F32), jax.ShapeDtypeStruct(lf_real_t.shape, F32)],
        name="prompt_fcum",
    )(lf_meta_t, lf_real_t)


def _softmax_pv(parts):
    m = None
    for s, _ in parts:
        mi = jnp.max(s, axis=-1, keepdims=True)
        m = mi if m is None else jnp.maximum(m, mi)
    l = None
    o = None
    for s, v in parts:
        p = jnp.exp(s - m)
        li = jnp.sum(p, axis=-1, keepdims=True)
        oi = jnp.dot(p.astype(BF16), v, preferred_element_type=F32)
        l = li if l is None else l + li
        o = oi if o is None else o + oi
    return o / l


def _causal(s, q0):
    r = lax.broadcasted_iota(jnp.int32, s.shape, 0) + q0
    c = lax.broadcasted_iota(jnp.int32, s.shape, 1)
    return jnp.where(c <= r, s, NEG_INF)


def _prompt_attn_tiles(q_ref, k_ref, v_ref, qm_ref, km_ref, vm_ref, fr_ref, fm_ref, o_ref, om_ref,
                        kb_ref, vb_ref, tiles, first):
    kmb = km_ref[0].astype(BF16)
    vmb = vm_ref[0].astype(BF16)
    f_meta = fm_ref[0][:, :N_META]

    if first:
        kb_ref[...] = k_ref[0].astype(BF16)
        vb_ref[...] = v_ref[0].astype(BF16)
        qmb = (qm_ref[0] * ATTN_SCALE).astype(BF16)
        sm = _causal(_dot_nt(qmb, kmb) - f_meta, 0)
        om_ref[...] = _softmax_pv([(sm, vmb)])

    for qi in tiles:
        q0 = qi * Q_TILE
        n = q0 + Q_TILE
        qb = (q_ref[0, q0:n, :] * ATTN_SCALE).astype(BF16)
        s_diag = _causal(_dot_nt(qb, kb_ref[q0:n, :]) - fr_ref[0, :, q0:n], 0)
        parts = [(s_diag, vb_ref[q0:n, :]), (_dot_nt(qb, kmb) - f_meta, vmb)]
        if q0 > 0:
            parts.append((_dot_nt(qb, kb_ref[:q0, :]) - fr_ref[0, :, :q0], vb_ref[:q0, :]))
        o_ref[q0:n, :] = _softmax_pv(parts)


def _decode_step(pt_ref, q_ref, kn_ref, vn_ref, lfn_ref, ck_ref, cv_ref, clf_ref, o_ref,
                 kbuf, vbuf, lfbuf, f_ref, sems, *, layer, n_pages):
    b = pl.program_id(0)
    nb = pl.num_programs(0)
    half = n_pages // 2
    n_heads = q_ref.shape[1]
    keys_half = half * PAGE_SIZE * n_heads
    n_chunks = keys_half // LANE
    slot = b % 2

    def copies(seq, slot_):
        out = []
        for j in range(n_pages):
            page = pt_ref[seq * n_pages + j]
            dst_lanes = pl.ds((j // half) * HEAD_DIM, HEAD_DIM)
            out.append(pltpu.make_async_copy(
                ck_ref.at[layer, page], kbuf.at[slot_, j % half, :, :, dst_lanes], sems.at[0, slot_]))
            out.append(pltpu.make_async_copy(
                cv_ref.at[layer, page], vbuf.at[slot_, j % half, :, :, dst_lanes], sems.at[1, slot_]))
            out.append(pltpu.make_async_copy(
                clf_ref.at[layer, page], lfbuf.at[slot_, j], sems.at[2, slot_]))
        return out

    @pl.when(b == 0)
    def _():
        for c in copies(0, 0):
            c.start()

    for c in copies(b, slot):
        c.wait()

    for c in copies(jnp.minimum(b + 1, nb - 1), 1 - slot):
        c.start()

    x = lfbuf[slot].reshape(n_pages * SUBLANE, LANE)
    nr = x.shape[0]
    w_in_row = _ones_where((LANE, LANE), lambda r, c: (r % n_heads == c % n_heads) & (r <= c))
    w_row_tot = _ones_where((LANE, LANE), lambda r, c: r % n_heads == c % n_heads)
    l_strict = _ones_where((nr, nr), lambda r, c: c < r)
    f_all = _dot3(x, w_in_row) + _dot3_rhs(l_strict, _dot3(x, w_row_tot))
    f_ref[...] = f_all

    q = q_ref[0] * ATTN_SCALE
    zero = jnp.zeros_like(q)
    q2 = jnp.concatenate([jnp.concatenate([q, zero], axis=1),
                          jnp.concatenate([zero, q], axis=1)], axis=0)
    kmat = kbuf[slot].reshape(keys_half, 2 * HEAD_DIM)
    s2 = _dot_nt(q2, kmat)

    f_lo = jnp.concatenate(
        [jnp.broadcast_to(f_ref[c:c + 1, :], (n_heads, LANE)) for c in range(n_chunks)], axis=1)
    f_hi = jnp.concatenate(
        [jnp.broadcast_to(f_ref[n_chunks + c:n_chunks + c + 1, :], (n_heads, LANE)) for c in range(n_chunks)], axis=1)
    f2 = jnp.concatenate([f_lo, f_hi], axis=0)

    row = lax.broadcasted_iota(jnp.int32, (n_heads, LANE), 0)
    col = lax.broadcasted_iota(jnp.int32, (n_heads, LANE), 1)
    bias = jnp.where(col % n_heads == row, 0.0, -jnp.inf)
    bias = jnp.concatenate([bias] * n_chunks, axis=1)
    bias = jnp.concatenate([bias, bias], axis=0)
    logit = s2 - f2 + bias
    m2 = jnp.max(logit, axis=-1, keepdims=True)
    m = jnp.maximum(m2[:n_heads], m2[n_heads:])
    p = jnp.exp(logit - jnp.concatenate([m, m], axis=0))
    l2 = jnp.sum(p, axis=-1, keepdims=True)
    l = l2[:n_heads] + l2[n_heads:]
    vmat = vbuf[slot].reshape(keys_half, 2 * HEAD_DIM)
    o2 = jnp.dot(p, vmat, preferred_element_type=F32)
    o = o2[:n_heads, :HEAD_DIM] + o2[n_heads:, HEAD_DIM:]

    last = jnp.broadcast_to(f_ref[nr - 1:nr, :], (n_heads, LANE))
    r8 = lax.broadcasted_iota(jnp.int32, (n_heads, LANE), 0)
    c8 = lax.broadcasted_iota(jnp.int32, (n_heads, LANE), 1)
    f_past = jnp.sum(jnp.where(c8 == LANE - n_heads + r8, last, 0.0), axis=-1, keepdims=True)
    s_new = jnp.sum(q * kn_ref[0], axis=-1, keepdims=True)
    logit_new = s_new - (f_past + lfn_ref[0])
    m_fin = jnp.maximum(m, logit_new)
    alpha = jnp.exp(m - m_fin)
    p_new = jnp.exp(logit_new - m_fin)
    o_ref[0] = (alpha * o + p_new * vn_ref[0]) / (alpha * l + p_new)

    @pl.when(b == nb - 1)
    def _():
        for c in copies(nb - 1, 1 - slot):
            c.wait()


def _mixed_attn_kernel(pt_ref, q_ref, kn_ref, vn_ref, lfn_ref, ck_ref, cv_ref, clf_ref,
                       pq_ref, pk_ref, pv_ref, pqm_ref, pkm_ref, pvm_ref, fr_ref, fm_ref,
                       o_ref, po_ref, pom_ref, kbuf, vbuf, lfbuf, f_ref, sems, pkb_ref, pvb_ref,
                       *, layer, n_pages, per):
    _decode_step(pt_ref, q_ref, kn_ref, vn_ref, lfn_ref, ck_ref, cv_ref, clf_ref, o_ref,
                 kbuf, vbuf, lfbuf, f_ref, sems, layer=layer, n_pages=n_pages)
    n_tiles = pq_ref.shape[1] // Q_TILE
    part = pl.program_id(0) % per
    for s_ in range(per):
        tiles = (s_, n_tiles - 1 - s_)

        @pl.when(part == s_)
        def _():
            _prompt_attn_tiles(pq_ref, pk_ref, pv_ref, pqm_ref, pkm_ref, pvm_ref, fr_ref, fm_ref,
                               po_ref, pom_ref, pkb_ref, pvb_ref, tiles, first=(s_ == 0))


def _mixed_attention(page_table, q, k_new, v_new, lf_new, cache_k, cache_v, cache_lf, layer,
                     z4, f_real, f_meta, batch, seq, meta_row0):
    n_seq, n_pages = page_table.shape
    _, n_heads, head_dim = q.shape
    half = n_pages // 2
    units = batch * n_heads
    per = n_seq // units
    assert n_pages % 2 == 0 and PAGE_SIZE * n_heads == SUBLANE * LANE and head_dim == HEAD_DIM
    assert n_seq == per * units and seq == 2 * per * Q_TILE
    meta_blk0 = meta_row0 // N_META

    def row_map(b, pt):
        return (b, 0, 0)

    def zspec(which, meta):
        if meta:
            return pl.BlockSpec((1, N_META, HEAD_DIM),
                                lambda b, pt: (which, meta_blk0 + (b // per) // n_heads, (b // per) % n_heads))
        return pl.BlockSpec((1, seq, HEAD_DIM), lambda b, pt: (which, (b // per) // n_heads, (b // per) % n_heads))

    def unit_map(b, pt):
        return ((b // per) // n_heads, (b // per) % n_heads)

    any_spec = pl.BlockSpec(memory_space=pl.ANY)
    grid_spec = pltpu.PrefetchScalarGridSpec(
        num_scalar_prefetch=1,
        grid=(n_seq,),
        in_specs=[
            pl.BlockSpec((1, n_heads, head_dim), row_map),
            pl.BlockSpec((1, n_heads, head_dim), row_map),
            pl.BlockSpec((1, n_heads, head_dim), row_map),
            pl.BlockSpec((1, n_heads, 1), row_map),
            any_spec, any_spec, any_spec,
            zspec(0, False), zspec(1, False), zspec(2, False),
            zspec(0, True), zspec(1, True), zspec(2, True),
            pl.BlockSpec((1, 1, seq), lambda b, pt: (b // per, 0, 0)),
            pl.BlockSpec((1, 1, LANE), lambda b, pt: (b // per, 0, 0)),
        ],
        out_specs=[
            pl.BlockSpec((1, n_heads, head_dim), row_map),
            pl.BlockSpec((seq, HEAD_DIM), unit_map),
            pl.BlockSpec((N_META, HEAD_DIM), unit_map),
        ],
        scratch_shapes=[
            pltpu.VMEM((2, half, PAGE_SIZE, n_heads, 2 * head_dim), F32),
            pltpu.VMEM((2, half, PAGE_SIZE, n_heads, 2 * head_dim), F32),
            pltpu.VMEM((2, n_pages, SUBLANE, LANE), F32),
            pltpu.VMEM((n_pages * SUBLANE, LANE), F32),
            pltpu.SemaphoreType.DMA((3, 2)),
            pltpu.VMEM((seq, HEAD_DIM), BF16),
            pltpu.VMEM((seq, HEAD_DIM), BF16),
        ],
    )
    return pl.pallas_call(
        functools.partial(_mixed_attn_kernel, layer=layer, n_pages=n_pages, per=per),
        grid_spec=grid_spec,
        out_shape=[
            jax.ShapeDtypeStruct((n_seq, n_heads, head_dim), F32),
            jax.ShapeDtypeStruct((batch * seq, n_heads * HEAD_DIM), F32),
            jax.ShapeDtypeStruct((batch * N_META, n_heads * HEAD_DIM), F32),
        ],
        compiler_params=pltpu.CompilerParams(
            dimension_semantics=("arbitrary",), vmem_limit_bytes=VMEM_LIMIT),
        name="mixed_attn",
    )(page_table.reshape(-1), q, k_new, v_new, lf_new, cache_k, cache_v, cache_lf,
      z4, z4, z4, z4, z4, z4, f_real, f_meta)


def _pool_project(d_groups, wp_ref, sc_ref, o_ref):
    for g, d in enumerate(d_groups):
        gw = d.shape[1]
        cs = slice(g * gw, (g + 1) * gw)
        o_ref[:, cs] = jnp.dot(d.astype(BF16), wp_ref[g], preferred_element_type=F32) * sc_ref[:, cs]


def _pool_real_kernel(halo_ref, cur_ref, wp_ref, sc_ref, o_ref, ext_ref):
    rows = cur_ref.shape[0]
    gw = cur_ref.shape[1] // len(POOL_WINDOWS)
    ext_ref[0:N_META, :] = halo_ref[...]
    ext_ref[N_META:, :] = cur_ref[...]
    ds = []
    for g, w in enumerate(POOL_WINDOWS):
        cs = slice(g * gw, (g + 1) * gw)
        cur = ext_ref[N_META:N_META + rows, cs]
        s = cur
        for back in range(1, w):
            s = s + ext_ref[N_META - back:N_META - back + rows, cs]
        ds.append(s * (1.0 / w) - cur)
    _pool_project(ds, wp_ref, sc_ref, o_ref)


def _pool_real(z4, w_pool, scale, batch, seq, meta_row0):
    width = z4.shape[2]
    tiles = seq // POOL_TILE
    per16 = POOL_TILE // N_META
    meta_blk0 = meta_row0 // N_META

    def halo_map(b, i):
        return (3, jnp.where(i == 0, meta_blk0 + b, (b * tiles + i) * per16 - 1), 0)

    return pl.pallas_call(
        _pool_real_kernel,
        grid=(batch, tiles),
        in_specs=[
            pl.BlockSpec((None, N_META, width), halo_map),
            pl.BlockSpec((None, POOL_TILE, width), lambda b, i: (3, b * tiles + i, 0)),
            pl.BlockSpec(w_pool.shape, lambda b, i: (0, 0, 0)),
            pl.BlockSpec((1, width), lambda b, i: (0, 0)),
        ],
        out_specs=pl.BlockSpec((POOL_TILE, width), lambda b, i: (b * tiles + i, 0)),
        out_shape=jax.ShapeDtypeStruct((batch * seq, width), F32),
        scratch_shapes=[pltpu.VMEM((N_META + POOL_TILE, width), F32)],
        compiler_params=pltpu.CompilerParams(
            dimension_semantics=("parallel", "parallel"), vmem_limit_bytes=VMEM_LIMIT),
        name="pool_real",
    )(z4, z4, w_pool, scale)


def _pool_small_kernel(pm_ref, ps_ref, st_ref, wp_ref, sc_ref, om_ref, os_ref, ext_ref):
    width = pm_ref.shape[1]
    gw = width // len(POOL_WINDOWS)
    ext_ref[0:N_META, :] = jnp.zeros((N_META, width), F32)
    pos = lax.broadcasted_iota(jnp.int32, (N_META, 1), 0)
    for b in range(pm_ref.shape[0] // N_META):
        ext_ref[N_META:, :] = pm_ref[b * N_META:(b + 1) * N_META, :]
        ds = []
        for g, w in enumerate(POOL_WINDOWS):
            cs = slice(g * gw, (g + 1) * gw)
            cur = ext_ref[N_META:2 * N_META, cs]
            s = cur
            for back in range(1, w):
                s = s + ext_ref[N_META - back:2 * N_META - back, cs]
            cnt = jnp.minimum(pos + 1, w).astype(F32)
            ds.append(s / cnt - cur)
        _pool_project(ds, wp_ref, sc_ref, om_ref.at[b * N_META:(b + 1) * N_META, :])
    ds = []
    for g, w in enumerate(POOL_WINDOWS):
        cs = slice(g * gw, (g + 1) * gw)
        cur = ps_ref[:, cs]
        s = cur
        for back in range(1, w):
            s = s + st_ref[POOL_CTX - back, :, cs]
        ds.append(s * (1.0 / w) - cur)
    _pool_project(ds, wp_ref, sc_ref, os_ref)


def _pool_small(p_meta, p_sample, state_t, w_pool, scale):
    return pl.pallas_call(
        _pool_small_kernel,
        out_shape=[jax.ShapeDtypeStruct(p_meta.shape, F32), jax.ShapeDtypeStruct(p_sample.shape, F32)],
        scratch_shapes=[pltpu.VMEM((2 * N_META, p_meta.shape[1]), F32)],
        compiler_params=pltpu.CompilerParams(vmem_limit_bytes=VMEM_LIMIT),
        name="pool_small",
    )(p_meta, p_sample, state_t, w_pool, scale)


def _outproj_kernel(h_ref, a_ref, pm_ref, at_ref, pmt_ref, w_ref, g_ref, o_ref):
    i = pl.program_id(0)
    n_real_tiles = pl.num_programs(0) - 1

    def project(a_src, pm_src):
        aw = a_src.shape[1]
        mix = _dot_mixed(a_src[...].astype(BF16), w_ref[0:aw, :])
        mix = mix + _dot_mixed(pm_src[...].astype(BF16), w_ref[aw:, :])
        o_ref[...] = h_ref[...] + _rms(mix, g_ref[...])

    @pl.when(i < n_real_tiles)
    def _():
        project(a_ref, pm_ref)

    @pl.when(i == n_real_tiles)
    def _():
        project(at_ref, pmt_ref)


def _out_proj(h, a_real, pm_real, a_tail, pm_tail, w_out, g, layer):
    rows, d = h.shape
    n_real = a_real.shape[0]
    tile = rows - n_real
    assert a_tail.shape[0] == tile and n_real % tile == 0
    n_real_tiles = n_real // tile

    def real_map(i):
        return (jnp.minimum(i, n_real_tiles - 1), 0)

    return pl.pallas_call(
        _outproj_kernel,
        grid=(n_real_tiles + 1,),
        in_specs=[
            pl.BlockSpec((tile, d), lambda i: (i, 0)),
            pl.BlockSpec((tile, a_real.shape[1]), real_map),
            pl.BlockSpec((tile, pm_real.shape[1]), real_map),
            pl.BlockSpec((tile, a_tail.shape[1]), lambda i: (0, 0)),
            pl.BlockSpec((tile, pm_tail.shape[1]), lambda i: (0, 0)),
            pl.BlockSpec((None,) + w_out.shape[1:], lambda i: (layer, 0, 0), pipeline_mode=pl.Buffered(1)),
            pl.BlockSpec((1, d), lambda i: (0, 0)),
        ],
        out_specs=pl.BlockSpec((tile, d), lambda i: (i, 0)),
        out_shape=jax.ShapeDtypeStruct((rows, d), F32),
        compiler_params=pltpu.CompilerParams(
            dimension_semantics=("arbitrary",), vmem_limit_bytes=VMEM_LIMIT),
        name="out_proj",
    )(h, a_real, pm_real, a_tail, pm_tail, w_out, g)


def _kv_assemble_kernel(*refs, depth, n_heads):
    z_refs = refs[:2 * depth]
    k_ref, v_ref = refs[2 * depth:]
    layer = pl.program_id(0)
    tensor = pl.program_id(2)
    part = pl.program_id(3)
    rows = z_refs[0].shape[0]

    def retile(real_ref, meta_ref, o_ref, p):
        for h in range(n_heads):
            lanes = slice(h * HEAD_DIM, (h + 1) * HEAD_DIM)
            if p == 0:
                o_ref[0:N_META, h, :] = meta_ref[:, lanes]
            o_ref[N_META + p * rows:N_META + (p + 1) * rows, h, :] = real_ref[:, lanes]

    for l in range(depth):
        for t, o_ref in enumerate((k_ref, v_ref)):
            for p in range(KV_PARTS):
                @pl.when((layer == l) & (tensor == t) & (part == p))
                def _():
                    retile(z_refs[2 * l], z_refs[2 * l + 1], o_ref, p)


def _kv_assemble(z4s, batch, seq, n_heads, meta_row0):
    depth = len(z4s)
    width = z4s[0].shape[2]
    meta_blk0 = meta_row0 // N_META
    rows = seq // KV_PARTS
    shape = jax.ShapeDtypeStruct((depth, batch, N_META + seq, n_heads, HEAD_DIM), F32)

    in_specs = []
    args = []
    for l, z in enumerate(z4s):
        def pick(l_, b, t, p, l=l):
            on = l_ == l
            early = l_ < l
            return (jnp.where(on, b, jnp.where(early, 0, batch - 1)),
                    jnp.where(on, t, jnp.where(early, 0, 1)),
                    jnp.where(on, p, jnp.where(early, 0, KV_PARTS - 1)))

        def real_map(l_, b, t, p, pick=pick):
            bb, tt, pp = pick(l_, b, t, p)
            return (1 + tt, bb * KV_PARTS + pp, 0)

        def meta_map(l_, b, t, p, pick=pick):
            bb, tt, _ = pick(l_, b, t, p)
            return (1 + tt, meta_blk0 + bb, 0)

        in_specs.append(pl.BlockSpec((None, rows, width), real_map))
        in_specs.append(pl.BlockSpec((None, N_META, width), meta_map))
        args += [z, z]

    out_spec = pl.BlockSpec((None, None, N_META + seq, n_heads, HEAD_DIM),
                            lambda l_, b, t, p: (l_, b, 0, 0, 0))
    return pl.pallas_call(
        functools.partial(_kv_assemble_kernel, depth=depth, n_heads=n_heads),
        grid=(depth, batch, 2, KV_PARTS),
        in_specs=in_specs,
        out_specs=[out_spec, out_spec],
        out_shape=[shape, shape],
        compiler_params=pltpu.CompilerParams(
            dimension_semantics=("arbitrary",) * 4, vmem_limit_bytes=VMEM_LIMIT),
        name="kv_assemble",
    )(*args)


def kernel(x_prompt, x_sample, cache_k, cache_v, cache_logf, state_pool, page_table, meta_tokens, w_in, b_forget, w_pool, pool_scale, w_out, ffn1_gate, ffn1_up, ffn1_down, ffn2_gate, ffn2_up, ffn2_down, g_pre_ffn1, g_post_ffn1, g_pre_mix, g_post_mix, g_pre_ffn2, g_post_ffn2):
    batch, seq, d_model = x_prompt.shape
    n_seq = x_sample.shape[0]
    depth, n_phys, page, n_heads, head_dim = cache_k.shape
    attn_w = n_heads * head_dim
    pool_w = state_pool.shape[-1]
    assert head_dim == HEAD_DIM and page == PAGE_SIZE and x_sample.shape[1] == 1
    assert attn_w == pool_w and w_in.shape[-1] == 3 * attn_w + pool_w + n_heads

    n_real = batch * seq
    s0 = n_real
    m0 = s0 + n_seq
    n_used = m0 + batch * N_META
    assert n_used <= ROW_ALIGN and m0 % N_META == 0 and seq % POOL_TILE == 0 and seq % Q_TILE == 0

    h = jnp.concatenate([
        x_prompt.reshape(n_real, d_model),
        x_sample.reshape(n_seq, d_model),
        jnp.tile(meta_tokens.astype(F32), (batch, 1)),
        jnp.zeros((ROW_ALIGN - n_used, d_model), F32)], axis=0)

    clf = cache_logf.reshape(depth, n_phys, SUBLANE, LANE)
    w_in_t = jnp.swapaxes(w_in, 1, 2)
    pad_rows = jnp.zeros((ROW_ALIGN - n_used, attn_w), F32)

    def row(v):
        return v.reshape(1, -1)

    outs = {k: [] for k in ("lfp", "pp", "ks", "vs", "lfs", "ps")}
    z4s = []
    for l in range(depth):
        w_f_t = jnp.pad(w_in_t[l, 3 * attn_w + pool_w:, :], ((0, LANE - n_heads), (0, 0)))
        b_f = jnp.pad(b_forget[l].astype(F32), (0, LANE - n_heads)).reshape(1, LANE)

        h = _ffn_half(h, row(g_pre_ffn1[l]), row(g_post_ffn1[l]), ffn1_gate, ffn1_up, ffn1_down, l)

        z4, lf = _in_proj(h, row(g_pre_mix[l]), w_in_t, w_f_t, b_f, n_heads, l)

        lf_real_t = lf[:n_real].reshape(batch, seq, n_heads).transpose(0, 2, 1).reshape(batch * n_heads, seq)
        lf_meta_t = lf[m0:n_used].reshape(batch, N_META, n_heads).transpose(0, 2, 1).reshape(batch * n_heads, N_META)
        lf_meta_t = jnp.pad(lf_meta_t, ((0, 0), (0, LANE - N_META)))
        f_meta, f_real = _prompt_fcum(lf_meta_t, lf_real_t)
        q_s = z4[0, s0:m0].reshape(n_seq, n_heads, head_dim)
        k_s = z4[1, s0:m0].reshape(n_seq, n_heads, head_dim)
        v_s = z4[2, s0:m0].reshape(n_seq, n_heads, head_dim)
        p_s = z4[3, s0:m0]
        lf_s = lf[s0:m0]
        a_s, a_real, a_meta = _mixed_attention(
            page_table, q_s, k_s, v_s, lf_s.reshape(n_seq, n_heads, 1), cache_k, cache_v, clf, l,
            z4, f_real.reshape(batch * n_heads, 1, seq), f_meta.reshape(batch * n_heads, 1, LANE),
            batch, seq, m0)
        a_s = a_s.reshape(n_seq, attn_w)

        wp = w_pool[l].astype(BF16)
        sc = row(pool_scale[l].astype(F32))
        pm_real = _pool_real(z4, wp, sc, batch, seq, m0)
        pm_meta, pm_s = _pool_small(z4[3, m0:n_used], p_s, state_pool[l].transpose(1, 0, 2), wp, sc)

        a_tail = jnp.concatenate([a_s, a_meta, pad_rows], axis=0)
        pm_tail = jnp.concatenate([pm_s, pm_meta, pad_rows], axis=0)
        h = _out_proj(h, a_real, pm_real, a_tail, pm_tail, w_out, row(g_post_mix[l]), l)

        h = _ffn_half(h, row(g_pre_ffn2[l]), row(g_post_ffn2[l]), ffn2_gate, ffn2_up, ffn2_down, l)

        def seq_major(x_real, x_meta, tail):
            return jnp.concatenate(
                [x_meta.reshape((batch, N_META) + tail), x_real.reshape((batch, seq) + tail)], axis=1)

        z4s.append(z4)
        outs["lfp"].append(seq_major(lf[:n_real], lf[m0:n_used], (n_heads,)))
        outs["pp"].append(jnp.stack([z4[3, (b + 1) * seq - POOL_CTX:(b + 1) * seq] for b in range(batch)], 0))
        outs["ks"].append(k_s.reshape(n_seq, 1, n_heads, head_dim))
        outs["vs"].append(v_s.reshape(n_seq, 1, n_heads, head_dim))
        outs["lfs"].append(lf_s.reshape(n_seq, 1, n_heads))
        outs["ps"].append(jnp.concatenate([state_pool[l][:, 1:], p_s[:, None, :]], axis=1))

    y_prompt = h[:n_real].reshape(batch, seq, d_model)
    y_sample = h[s0:m0].reshape(n_seq, 1, d_model)
    st = {k: jnp.stack(v, 0) for k, v in outs.items()}
    k_prompt, v_prompt = _kv_assemble(z4s, batch, seq, n_heads, m0)
    return (y_prompt, y_sample, k_prompt, v_prompt, st["lfp"], st["pp"],
            st["ks"], st["vs"], st["lfs"], st["ps"])
```
